```python
import jax, jax.numpy as jnp
from jax import lax
import numpy as np

D_MODEL = 1024
BATCH = 2
SEQ = 8192
DEPTH = 1

CTX_LEN = 256
GRID_W = 64
D_FOURIER = D_MODEL // 2
FOURIER_GROUPS = 4
FOURIER_GROUP_DIM = D_FOURIER // FOURIER_GROUPS
D_REC = D_MODEL - D_FOURIER
REC_HEADS = 4
REC_HEAD_DIM = D_REC // REC_HEADS
CHUNK = 64
SUB_CHUNK = 16
N_EXPERTS = 16
EC_CAPACITY_FACTOR = 2
D_EXPERT = 2816
NORM_EPS = 1e-6
D_IN = D_FOURIER + 5 * D_REC
N_ADA = 6

kernel_name = "hybrid_fnet_hgrn2_ec_moe_dit"


def rms_norm(x, gain):
    xf = x.astype(jnp.float32)
    y = xf * lax.rsqrt(jnp.mean(xf * xf, axis=-1, keepdims=True) + NORM_EPS)
    return (y * gain.astype(jnp.float32)).astype(x.dtype)


def adaln(cond, w, b):
    mod = (jax.nn.silu(cond) @ w + b)[:, None, :]
    return jnp.split(mod, N_ADA, axis=-1)


def modulate(h, shift, scale):
    return h * (1 + scale) + shift


def fourier_mix(u, w_f):
    b_, l_, _ = u.shape
    z = u.astype(jnp.float32).reshape(b_, l_, FOURIER_GROUPS, FOURIER_GROUP_DIM)
    z = jnp.fft.fft2(z, axes=(1, 3), norm="ortho").real.astype(u.dtype)
    y = jnp.einsum('blgc,gcd->blgd', z, w_f)
    return y.reshape(b_, l_, D_FOURIER)


def _to_chunks(t):
    b_, l_, h_, d_ = t.shape
    return t.reshape(b_, l_ // CHUNK, CHUNK, h_, d_).transpose(1, 0, 3, 2, 4)


def _from_chunks(t):
    n_, b_, h_, c_, d_ = t.shape
    return t.transpose(1, 0, 3, 2, 4).reshape(b_, n_ * c_, h_, d_)


def _chunk_step(S, inp):
    qc, kc, vc, lc = inp
    b_, h_, c_, d_ = qc.shape
    ns = c_ // SUB_CHUNK
    bcum = jnp.cumsum(lc, axis=2)
    o_inter = jnp.einsum('bhcd,bhde->bhce', qc * jnp.exp(bcum), S)
    bs = bcum.reshape(b_, h_, ns, SUB_CHUNK, d_)
    qs = qc.reshape(b_, h_, ns, SUB_CHUNK, d_)
    ks = kc.reshape(b_, h_, ns, SUB_CHUNK, d_)
    vs = vc.reshape(b_, h_, ns, SUB_CHUNK, -1)
    b_ref = jnp.concatenate([jnp.zeros_like(bs[:, :, :1, 0]), bs[:, :, :-1, -1]], axis=2)
    q_hat = qs * jnp.exp(bs - b_ref[:, :, :, None, :])
    prev = jnp.arange(c_)[None, :] < (jnp.arange(ns) * SUB_CHUNK)[:, None]
    k_hat = kc[:, :, None] * jnp.exp(jnp.where(prev[:, :, None],
                                               b_ref[:, :, :, None, :] - bcum[:, :, None], -jnp.inf))
    a_off = jnp.einsum('bhjtd,bhjsd->bhjts', q_hat, k_hat)
    tri = jnp.tril(jnp.ones((SUB_CHUNK, SUB_CHUNK), dtype=bool))
    decay = jnp.exp(jnp.where(tri[:, :, None],
                              bs[:, :, :, :, None, :] - bs[:, :, :, None, :, :], -jnp.inf))
    a_diag = jnp.einsum('bhjtd,bhjtsd,bhjsd->bhjts', qs, decay, ks)
    o_intra = (jnp.einsum('bhjts,bhse->bhjte', a_off, vc)
               + jnp.einsum('bhjts,bhjse->bhjte', a_diag, vs))
    o = o_inter + o_intra.reshape(b_, h_, c_, -1)
    b_last = bcum[:, :, -1]
    S_new = (jnp.exp(b_last)[..., None] * S
             + jnp.einsum('bhcd,bhce->bhde', kc * jnp.exp(b_last[:, :, None] - bcum), vc))
    return S_new, o


def scan_dir(q, k, v, logf, S0, reverse):
    if reverse:
        q, k, v, logf = (jnp.flip(t, axis=1) for t in (q, k, v, logf))
    S_fin, o = lax.scan(_chunk_step, S0, (_to_chunks(q), _to_chunks(k), _to_chunks(v), _to_chunks(logf)))
    o = _from_chunks(o)
    if reverse:
        o = jnp.flip(o, axis=1)
    return o, S_fin


def split_proj(p, lb):
    b_, l_, _ = p.shape
    u, q, v, f_f, f_b, g = jnp.split(p, [D_FOURIER + i * D_REC for i in range(5)], axis=-1)
    heads = lambda t: t.astype(jnp.float32).reshape(b_, l_, REC_HEADS, REC_HEAD_DIM)

    def gates(fp, lbd):
        lbd = lbd.reshape(REC_HEADS, REC_HEAD_DIM)
        fg = lbd + (1 - lbd) * jax.nn.sigmoid(heads(fp))
        return jnp.log(fg), 1 - fg

    return u, jax.nn.silu(heads(q)), heads(v), gates(f_f, lb[0]), gates(f_b, lb[1]), g


def rec_mixer(q, v, gf, gb, g, S0_f, S0_b, gain):
    o_f, S_f = scan_dir(q, gf[1], v, gf[0], S0_f, False)
    o_b, S_b = scan_dir(q, gb[1], v, gb[0], S0_b, True)
    o = o_f + o_b
    o = o * lax.rsqrt(jnp.mean(o * o, axis=-1, keepdims=True) + NORM_EPS)
    o = o * gain.astype(jnp.float32).reshape(REC_HEADS, REC_HEAD_DIM)
    b_, l_ = o.shape[:2]
    y = o.reshape(b_, l_, D_REC).astype(g.dtype) * jax.nn.silu(g)
    return y, S_f, S_b


def ec_moe(h, w_r, w_g, w_u, w_d):
    b_, l_, d_ = h.shape
    cap = EC_CAPACITY_FACTOR * l_ // N_EXPERTS
    affinity = jax.nn.softmax((h @ w_r).astype(jnp.float32), axis=-1)
    gate, idx = lax.top_k(jnp.swapaxes(affinity, 1, 2), cap)
    xs = jax.vmap(lambda hb, ib: hb[ib])(h, idx)
    hid = jax.nn.silu(jnp.einsum('becd,edf->becf', xs, w_g)) * jnp.einsum('becd,edf->becf', xs, w_u)
    ye = jnp.einsum('becf,efd->becd', hid, w_d) * gate[..., None].astype(h.dtype)
    flat = (idx + jnp.arange(b_)[:, None, None] * l_).reshape(-1)
    out = jax.ops.segment_sum(ye.reshape(-1, d_), flat, num_segments=b_ * l_)
    return out.reshape(b_, l_, d_)


def setup_inputs(seed: int = 0) -> dict:
    key = jax.random.key(seed)
    ks = jax.random.split(key, 20)
    nrm = lambda k, shape, s: jax.random.normal(k, shape, jnp.float32) * s
    D = D_MODEL
    return {
        "x": nrm(ks[0], (BATCH, SEQ, D), 1.0),
        "c": nrm(ks[1], (BATCH, D), 1.0),
        "ctx": nrm(ks[2], (BATCH, CTX_LEN, D), 1.0),
        "c_ctx": nrm(ks[3], (D,), 1.0),
        "w_ada": nrm(ks[4], (DEPTH, D, N_ADA * D), 0.5 * D ** -0.5),
        "b_ada": nrm(ks[5], (DEPTH, N_ADA * D), 0.01),
        "g_mix": 1.0 + nrm(ks[6], (DEPTH, D), 0.02),
        "w_in": nrm(ks[7], (DEPTH, D, D_IN), D ** -0.5),
        "w_fourier": nrm(ks[8], (DEPTH, FOURIER_GROUPS, FOURIER_GROUP_DIM, FOURIER_GROUP_DIM), FOURIER_GROUP_DIM ** -0.5),
        "lb_logits": nrm(ks[9], (DEPTH + 1, 2, D_REC), 0.5),
        "g_rec": 1.0 + nrm(ks[10], (DEPTH, D_REC), 0.02),
        "w_out": nrm(ks[11], (DEPTH, D, D), D ** -0.5),
        "g_ffn": 1.0 + nrm(ks[12], (DEPTH, D), 0.02),
        "w_router": nrm(ks[13], (DEPTH, D, N_EXPERTS), D ** -0.5),
        "w_exp_gate": nrm(ks[14], (DEPTH, N_EXPERTS, D, D_EXPERT), D ** -0.5),
        "w_exp_up": nrm(ks[15], (DEPTH, N_EXPERTS, D, D_EXPERT), D ** -0.5),
        "w_exp_down": nrm(ks[16], (DEPTH, N_EXPERTS, D_EXPERT, D), D_EXPERT ** -0.5),
        "g_final": 1.0 + nrm(ks[17], (D,), 0.02),
    }


def reference(x, c, ctx, c_ctx, w_ada, b_ada, g_mix, w_in, w_fourier, lb_logits, g_rec, w_out,
              g_ffn, w_router, w_exp_gate, w_exp_up, w_exp_down, g_final):
    b_ = x.shape[0]
    lb_all = jnp.cumsum(jax.nn.softmax(lb_logits.astype(jnp.float32), axis=0), axis=0)
    for layer in range(DEPTH):
        ctx_out_needed = layer < DEPTH - 1
        sh1, sc1, gt1, sh2, sc2, gt2 = adaln(c, w_ada[layer], b_ada[layer])
        sh1c, sc1c, gt1c, sh2c, sc2c, gt2c = adaln(c_ctx[None], w_ada[layer], b_ada[layer])
        lb = lb_all[layer]

        h_ctx = modulate(rms_norm(ctx, g_mix[layer]), sh1c, sc1c)
        h_lat = modulate(rms_norm(x, g_mix[layer]), sh1, sc1)
        u_c, q_c, v_c, gf_c, gb_c, g_c = split_proj(h_ctx @ w_in[layer], lb)
        u_l, q_l, v_l, gf_l, gb_l, g_l = split_proj(h_lat @ w_in[layer], lb)

        S_zero = jnp.zeros((b_, REC_HEADS, REC_HEAD_DIM, REC_HEAD_DIM), jnp.float32)
        rec_c, S_cf, S_cb = rec_mixer(q_c, v_c, gf_c, gb_c, g_c, S_zero, S_zero, g_rec[layer])
        rec_l, _, _ = rec_mixer(q_l, v_l, gf_l, gb_l, g_l, S_cf, S_cb, g_rec[layer])

        mix_l = jnp.concatenate([fourier_mix(u_l, w_fourier[layer]), rec_l], axis=-1) @ w_out[layer]
        x = x + gt1 * mix_l
        if ctx_out_needed:
            mix_c = jnp.concatenate([fourier_mix(u_c, w_fourier[layer]), rec_c], axis=-1) @ w_out[layer]
            ctx = ctx + gt1c * mix_c

        x = x + gt2 * ec_moe(modulate(rms_norm(x, g_ffn[layer]), sh2, sc2), w_router[layer],
                             w_exp_gate[layer], w_exp_up[layer], w_exp_down[layer])
        if ctx_out_needed:
            ctx = ctx + gt2c * ec_moe(modulate(rms_norm(ctx, g_ffn[layer]), sh2c, sc2c), w_router[layer],
                                      w_exp_gate[layer], w_exp_up[layer], w_exp_down[layer])
    return rms_norm(x, g_final)
```

```python
import functools

import numpy as np
import jax
import jax.numpy as jnp
from jax import lax
from jax.experimental import pallas as pl
from jax.experimental.pallas import tpu as pltpu

F32 = jnp.float32
BF16 = jnp.bfloat16
I32 = jnp.int32

NORM_EPS = 1e-6
LANE = 128
N_ADA = 6
FOURIER_GROUPS = 4
GROUP_DIM = 128
REC_HEADS = 4
HEAD_DIM = 128
N_EXPERTS = 16
EC_CAPACITY_FACTOR = 2
DFT_ROWS = 128
DFT_COLS = 64
REC_CHUNK = 128
VMEM_LIMIT = 56 * 1024 * 1024


def _dot(a, b):
    return jnp.dot(a, b, preferred_element_type=F32)


def _dot_nt(a, b):
    return lax.dot_general(a, b, (((1,), (1,)), ((), ())), preferred_element_type=F32)


def _dot_tn(a, b):
    return lax.dot_general(a, b, (((0,), (0,)), ((), ())), preferred_element_type=F32)


def _split(x):
    hi = x.astype(BF16)
    lo = (x - hi.astype(F32)).astype(BF16)
    return hi, lo


def _dot3(a, b):
    a_hi, a_lo = _split(a)
    b_hi, b_lo = _split(b)
    return _dot(a_hi, b_hi) + _dot(a_lo, b_hi) + _dot(a_hi, b_lo)


def _silu(x):
    return x * jax.nn.sigmoid(x)


def _rms(x):
    return x * lax.rsqrt(jnp.mean(x * x, axis=-1, keepdims=True) + NORM_EPS)


def _bf16_table(a):
    return jnp.asarray(a, F32).astype(BF16)


def _ada_kernel(cond_ref, w_ref, b_ref, o_ref):
    o_ref[...] = _dot3(_silu(cond_ref[...]), w_ref[...]) + b_ref[...]


def _ada(cond, w, b):
    d = cond.shape[1]
    n = w.shape[1]
    return pl.pallas_call(
        _ada_kernel,
        out_shape=jax.ShapeDtypeStruct((cond.shape[0], n), F32),
        grid=(n // d,),
        in_specs=[pl.BlockSpec(cond.shape, lambda j: (0, 0)),
                  pl.BlockSpec((d, d), lambda j: (0, j)),
                  pl.BlockSpec((1, d), lambda j: (0, j))],
        out_specs=pl.BlockSpec((cond.shape[0], d), lambda j: (0, j)),
        compiler_params=pltpu.CompilerParams(vmem_limit_bytes=VMEM_LIMIT),
        name="ada",
    )(cond, w, b.reshape(1, n))


def _inproj_kernel(x_ref, mod_ref, gmix_ref, w_ref, lbl_ref, *out_refs, ctx_row, parts):
    d = x_ref.shape[2]
    row = ctx_row if ctx_row is not None else pl.program_id(0)
    shift = mod_ref[pl.ds(row, 1), 0:d]
    scale = mod_ref[pl.ds(row, 1), d:2 * d]
    h = _rms(x_ref[0]) * gmix_ref[...] * (1.0 + scale) + shift
    hb = h.astype(BF16)

    l0 = lbl_ref[0]
    l1 = lbl_ref[1]
    m = jnp.maximum(l0, l1)
    e0 = jnp.exp(l0 - m)
    e1 = jnp.exp(l1 - m)
    lb = e0 / (e0 + e1)

    for o_ref, (kind, col) in zip(out_refs, parts):
        w = o_ref.shape[2]
        p = _dot(hb, w_ref[:, col:col + w])
        if kind == "raw":
            o_ref[0] = p
        elif kind == "silu":
            o_ref[0] = _silu(p)
        else:
            lbd = lb[0:1] if kind == "logf_fwd" else lb[1:2]
            o_ref[0] = jnp.log(lbd + (1.0 - lbd) * jax.nn.sigmoid(p))


def _inproj(x, mod, g_mix, w_bf16, lb_logits, *, ctx_row, parts, tm):
    b, l, d = x.shape
    width = 512
    kern = functools.partial(_inproj_kernel, ctx_row=ctx_row, parts=parts)
    return pl.pallas_call(
        kern,
        out_shape=[jax.ShapeDtypeStruct((b, l, width), F32) for _ in parts],
        grid=(b, l // tm),
        in_specs=[pl.BlockSpec((1, tm, d), lambda i, j: (i, j, 0)),
                  pl.BlockSpec(mod.shape, lambda i, j: (0, 0)),
                  pl.BlockSpec((1, d), lambda i, j: (0, 0)),
                  pl.BlockSpec(w_bf16.shape, lambda i, j: (0, 0)),
                  pl.BlockSpec(lb_logits.shape, lambda i, j: (0, 0, 0))],
        out_specs=[pl.BlockSpec((1, tm, width), lambda i, j: (i, j, 0)) for _ in parts],
        compiler_params=pltpu.CompilerParams(
            dimension_semantics=("parallel", "parallel"), vmem_limit_bytes=VMEM_LIMIT),
        name="inproj_ctx" if ctx_row is not None else "inproj",
    )(x, mod, g_mix.reshape(1, d), w_bf16, lb_logits)


def _dft_tables():
    n1, n2 = DFT_ROWS, DFT_COLS
    k1 = np.arange(n1)
    a1 = 2.0 * np.pi * np.outer(k1, k1) / n1
    f1 = np.concatenate([np.cos(a1), -np.sin(a1)], axis=0)
    c = np.arange(n2)
    th = 2.0 * np.pi * np.outer(c, k1) / (n1 * n2)
    twc = np.repeat(np.cos(th)[:, :, None], LANE, axis=2)
    tws = np.repeat(np.sin(th)[:, :, None], LANE, axis=2)
    a2 = 2.0 * np.pi * np.outer(c, c) / n2
    f3a = np.concatenate([np.cos(a2), -np.sin(a2)], axis=0)
    f3b = np.concatenate([np.sin(a2), np.cos(a2)], axis=0)
    return (_bf16_table(f1), jnp.asarray(twc, F32), jnp.asarray(tws, F32),
            _bf16_table(f3a), _bf16_table(f3b))


def _dft1_kernel(x_ref, f1_ref, twc_ref, tws_ref, o_ref):
    n1 = x_ref.shape[1]
    ch = x_ref.shape[3]
    reps = ch // LANE
    for cc in range(x_ref.shape[2]):
        xc = x_ref[0, :, cc, :].astype(BF16)
        g = _dot(f1_ref[...], xc)
        tre = g[:n1]
        tim = g[n1:]
        cw = jnp.concatenate([twc_ref[cc]] * reps, axis=1)
        sw = jnp.concatenate([tws_ref[cc]] * reps, axis=1)
        o_ref[0, :, cc, 0:ch] = tre * cw + tim * sw
        o_ref[0, :, cc, ch:2 * ch] = tim * cw - tre * sw


def _dft1(u4, f1, twc, tws):
    b, n1, n2, ch = u4.shape
    cb = 8
    return pl.pallas_call(
        _dft1_kernel,
        out_shape=jax.ShapeDtypeStruct((b, n1, n2, 2 * ch), F32),
        grid=(b, n2 // cb),
        in_specs=[pl.BlockSpec((1, n1, cb, ch), lambda i, j: (i, 0, j, 0)),
                  pl.BlockSpec(f1.shape, lambda i, j: (0, 0)),
                  pl.BlockSpec((cb, n1, LANE), lambda i, j: (j, 0, 0)),
                  pl.BlockSpec((cb, n1, LANE), lambda i, j: (j, 0, 0))],
        out_specs=pl.BlockSpec((1, n1, cb, 2 * ch), lambda i, j: (i, 0, j, 0)),
        compiler_params=pltpu.CompilerParams(
            dimension_semantics=("parallel", "parallel"), vmem_limit_bytes=VMEM_LIMIT),
        name="dft1",
    )(u4, f1, twc, tws)


def _dft3_kernel(t_ref, f3a_ref, f3b_ref, o_ref):
    n2 = t_ref.shape[2]
    ch = t_ref.shape[3] // 2
    for kk in range(t_ref.shape[1]):
        t = t_ref[0, kk]
        x = _dot(f3a_ref[...], t[:, :ch].astype(BF16)) + _dot(f3b_ref[...], t[:, ch:].astype(BF16))
        o_ref[0, :, kk, 0:ch] = x[:n2]
        o_ref[0, :, kk, ch:2 * ch] = x[n2:]


def _dft3(t4, f3a, f3b):
    b, n1, n2, w = t4.shape
    kb = 8
    return pl.pallas_call(
        _dft3_kernel,
        out_shape=jax.ShapeDtypeStruct((b, n2, n1, w), F32),
        grid=(b, n1 // kb),
        in_specs=[pl.BlockSpec((1, kb, n2, w), lambda i, j: (i, j, 0, 0)),
                  pl.BlockSpec(f3a.shape, lambda i, j: (0, 0)),
                  pl.BlockSpec(f3b.shape, lambda i, j: (0, 0))],
        out_specs=pl.BlockSpec((1, n2, kb, w), lambda i, j: (i, 0, j, 0)),
        compiler_params=pltpu.CompilerParams(
            dimension_semantics=("parallel", "parallel"), vmem_limit_bytes=VMEM_LIMIT),
        name="dft3",
    )(t4, f3a, f3b)


def _rec_tables(reverse, full):
    c = REC_CHUNK
    t = np.arange(c)[:, None]
    s = np.arange(c)[None, :]
    mats, masks = [], [np.eye(c)]
    if full:
        m = 1
        while m < c:
            blk = t // m
            odd = (blk % 2) == 1
            start = blk * m
            end = blk * m + m - 1
            w = np.where(odd, (s >= start) & (s <= t), (s > t) & (s <= end))
            mats.append(w.astype(np.float64))
            masks.append(((t // (2 * m) == s // (2 * m)) & odd & ((s // m) % 2 == 0)).astype(np.float64))
            m *= 2
        mats.append((s <= t).astype(np.float64))
    mats.append((s > t).astype(np.float64))
    if reverse:
        mats = [a[::-1, ::-1] for a in mats]
        masks = [a[::-1, ::-1] for a in masks]
    return jnp.asarray(np.concatenate(mats, axis=0), BF16), jnp.asarray(np.stack(masks), F32)


def _rec_direction(q, v, lc, st_ref, wall_ref, mask_ref, emit_out):
    c = REC_CHUNK
    k = 1.0 - jnp.exp(lc)
    hi, lo = _split(lc)
    res = _dot(wall_ref[...], jnp.concatenate([hi, lo], axis=1))
    dec = jnp.exp(res[:, :HEAD_DIM] + res[:, HEAD_DIM:])
    st = st_ref[...]
    vb = v.astype(BF16)
    o = None
    if emit_out:
        nl = wall_ref.shape[0] // c - 2
        a = mask_ref[0] * _dot_nt(q.astype(BF16), k.astype(BF16))
        for i in range(nl):
            dm = dec[i * c:(i + 1) * c]
            a = a + mask_ref[i + 1] * _dot_nt((q * dm).astype(BF16), (k * dm).astype(BF16))
        qd = (q * dec[nl * c:(nl + 1) * c]).astype(BF16)
        o = _dot_nt(qd, st.astype(BF16)) + _dot(a.astype(BF16), vb)
    kd = (k * dec[-c:]).astype(BF16)
    tot = jnp.exp(jnp.sum(lc, axis=0, keepdims=True))
    st_ref[...] = st * tot + _dot_tn(vb, kd)
    return o


def _rec_kernel(qf_ref, vf_ref, lf_ref, qb_ref, vb_ref, lb_ref, s0f_ref, s0b_ref,
                wf_ref, mf_ref, wb_ref, mb_ref, of_ref, ob_ref, sf_ref, sb_ref, *, emit_out):
    n = pl.program_id(2)

    @pl.when(n == 0)
    def _():
        sf_ref[...] = s0f_ref[0, 0]
        sb_ref[...] = s0b_ref[0, 0]

    o_f = _rec_direction(qf_ref[0], vf_ref[0], lf_ref[0], sf_ref, wf_ref, mf_ref, emit_out)
    o_b = _rec_direction(qb_ref[0], vb_ref[0], lb_ref[0], sb_ref, wb_ref, mb_ref, emit_out)
    if emit_out:
        of_ref[0] = o_f
        ob_ref[0] = o_b
    else:
        @pl.when(n == pl.num_programs(2) - 1)
        def _():
            of_ref[0, 0] = sf_ref[...]
            ob_ref[0, 0] = sb_ref[...]


def _rec(q, v, lcf, lcb, s0f, s0b, *, emit_out):
    b, l, _ = q.shape
    c = REC_CHUNK
    n = l // c
    wf, mf = _rec_tables(False, emit_out)
    wb, mb = _rec_tables(True, emit_out)
    fwd = lambda i, h, j: (i, j, h)
    bwd = lambda i, h, j: (i, n - 1 - j, h)
    tok = lambda im: pl.BlockSpec((1, c, HEAD_DIM), im)
    state = pl.BlockSpec((1, 1, HEAD_DIM, HEAD_DIM), lambda i, h, j: (i, h, 0, 0))
    const2 = lambda a: pl.BlockSpec(a.shape, lambda i, h, j: (0, 0))
    const3 = lambda a: pl.BlockSpec(a.shape, lambda i, h, j: (0, 0, 0))
    if emit_out:
        out_shape = [jax.ShapeDtypeStruct((b, l, REC_HEADS * HEAD_DIM), F32)] * 2
        out_specs = [tok(fwd), tok(bwd)]
    else:
        out_shape = [jax.ShapeDtypeStruct((b, REC_HEADS, HEAD_DIM, HEAD_DIM), F32)] * 2
        out_specs = [state, state]
    return pl.pallas_call(
        functools.partial(_rec_kernel, emit_out=emit_out),
        out_shape=out_shape,
        grid=(b, REC_HEADS, n),
        in_specs=[tok(fwd), tok(fwd), tok(fwd), tok(bwd), tok(bwd), tok(bwd), state, state,
                  const2(wf), const3(mf), const2(wb), const3(mb)],
        out_specs=out_specs,
        scratch_shapes=[pltpu.VMEM((HEAD_DIM, HEAD_DIM), F32), pltpu.VMEM((HEAD_DIM, HEAD_DIM), F32)],
        compiler_params=pltpu.CompilerParams(
            dimension_semantics=("parallel", "parallel", "arbitrary"), vmem_limit_bytes=VMEM_LIMIT),
        name="rec" if emit_out else "rec_ctx",
    )(q, v, lcf, q, v, lcb, s0f, s0b, wf, mf, wb, mb)


def _channel_dft_tables(scale):
    k = np.arange(GROUP_DIM)
    a = 2.0 * np.pi * np.outer(k, k) / GROUP_DIM
    return _bf16_table(np.cos(a) * scale), _bf16_table(np.sin(a) * scale)


def _outproj_kernel(xs_ref, of_ref, ob_ref, sg_ref, x_ref, mod_ref, grec_ref, cc_ref, sc_ref,
                    wf_ref, wo_ref, gffn_ref, wr_ref, x1_ref, h2_ref, aff_ref, *, n_experts):
    d = x_ref.shape[2]
    half = xs_ref.shape[2] // 2
    row = pl.program_id(0)
    gate1 = mod_ref[pl.ds(row, 1), 2 * d:3 * d]
    shift2 = mod_ref[pl.ds(row, 1), 3 * d:4 * d]
    scale2 = mod_ref[pl.ds(row, 1), 4 * d:5 * d]

    parts = []
    for g in range(FOURIER_GROUPS):
        lo = g * GROUP_DIM
        xre = xs_ref[0, :, lo:lo + GROUP_DIM].astype(BF16)
        xim = xs_ref[0, :, half + lo:half + lo + GROUP_DIM].astype(BF16)
        rez = _dot(xre, cc_ref[...]) + _dot(xim, sc_ref[...])
        parts.append(_dot(rez.astype(BF16), wf_ref[g]).astype(BF16))
    o = of_ref[0] + ob_ref[0]
    for h in range(REC_HEADS):
        lo = h * HEAD_DIM
        oh = _rms(o[:, lo:lo + HEAD_DIM]) * grec_ref[:, lo:lo + HEAD_DIM]
        parts.append((oh * sg_ref[0, :, lo:lo + HEAD_DIM]).astype(BF16))
    mix = _dot(jnp.concatenate(parts, axis=1), wo_ref[...])
    x1 = x_ref[0] + gate1 * mix
    x1_ref[0] = x1

    h2 = _rms(x1) * gffn_ref[...] * (1.0 + scale2) + shift2
    for s in range(d // LANE):
        h2_ref[0, :, s, :] = h2[:, s * LANE:(s + 1) * LANE]
    logits = _dot3(h2, wr_ref[...])
    lane = lax.broadcasted_iota(I32, logits.shape, 1)
    logits = jnp.where(lane < n_experts, logits, -1e30)
    e = jnp.exp(logits - jnp.max(logits, axis=-1, keepdims=True))
    aff = e / jnp.sum(e, axis=-1, keepdims=True)
    aff_ref[0] = aff.T[:n_experts]


def _outproj(xs, o_f, o_b, sg, x, mod, g_rec, w_fourier_bf16, w_out_bf16, g_ffn, w_router_pad, *, n_experts, tm):
    b, l, d = x.shape
    cc, sc = _channel_dft_tables(1.0 / np.sqrt(l * GROUP_DIM))
    tokens = lambda w: pl.BlockSpec((1, tm, w), lambda i, j: (i, j, 0))
    full = lambda a: pl.BlockSpec(a.shape, lambda i, j: (0,) * a.ndim)
    g_rec2 = g_rec.reshape(1, -1)
    g_ffn2 = g_ffn.reshape(1, d)
    return pl.pallas_call(
        functools.partial(_outproj_kernel, n_experts=n_experts),
        out_shape=[jax.ShapeDtypeStruct((b, l, d), F32),
                   jax.ShapeDtypeStruct((b, l, d // LANE, LANE), F32),
                   jax.ShapeDtypeStruct((b, n_experts, l), F32)],
        grid=(b, l // tm),
        in_specs=[tokens(xs.shape[2]), tokens(o_f.shape[2]), tokens(o_b.shape[2]), tokens(sg.shape[2]),
                  tokens(d), full(mod), full(g_rec2), full(cc), full(sc), full(w_fourier_bf16),
                  full(w_out_bf16), full(g_ffn2), full(w_router_pad)],
        out_specs=[tokens(d),
                   pl.BlockSpec((1, tm, d // LANE, LANE), lambda i, j: (i, j, 0, 0)),
                   pl.BlockSpec((1, n_experts, tm), lambda i, j: (i, 0, j))],
        compiler_params=pltpu.CompilerParams(
            dimension_semantics=("parallel", "parallel"), vmem_limit_bytes=VMEM_LIMIT),
        name="outproj",
    )(xs, o_f, o_b, sg, x, mod, g_rec2, cc, sc, w_fourier_bf16, w_out_bf16, g_ffn2, w_router_pad)


def _route_kernel(aff_ref, tri_ref, idx_ref, cum_ref, *, cap, jb):
    a = aff_ref[0]
    ne, l = a.shape

    def value_bit(i, thr):
        cand = thr | jnp.left_shift(jnp.int32(1), 30 - i)
        cnt = jnp.sum(jnp.where(a >= pltpu.bitcast(cand, F32), 1.0, 0.0), axis=1, keepdims=True)
        return jnp.where(cnt >= cap, cand, thr)

    thr = lax.fori_loop(0, 31, value_bit, jnp.zeros((ne, 1), I32))
    gt = a >= pltpu.bitcast(thr + 1, F32)
    eq = (a >= pltpu.bitcast(thr, F32)) & jnp.logical_not(gt)
    need = cap - jnp.sum(jnp.where(gt, 1.0, 0.0), axis=1, keepdims=True)
    pos = lax.broadcasted_iota(I32, (ne, l), 1)
    nbits = int(l).bit_length()

    def index_bit(i, x):
        cand = x + jnp.left_shift(jnp.int32(1), nbits - 1 - i)
        cnt = jnp.sum(jnp.where(eq & (pos < cand), 1.0, 0.0), axis=1, keepdims=True)
        return jnp.where(cnt < need, cand, x)

    last = lax.fori_loop(0, nbits, index_bit, jnp.zeros((ne, 1), I32))
    sel = (gt | (eq & (pos <= last))).astype(BF16)

    off = jnp.zeros((ne, 1), F32)
    for c in range(l // LANE):
        p = _dot(sel[:, c * LANE:(c + 1) * LANE], tri_ref[...]) + off
        cum_ref[:, c * LANE:(c + 1) * LANE] = p
        off = p[:, LANE - 1:LANE]

    ones = jnp.ones((8, LANE), BF16)
    for e in range(ne):
        for j0 in range(0, cap, jb):
            slot = (lax.broadcasted_iota(I32, (jb, LANE), 0) + j0).astype(F32)

            def tile(c, acc):
                start = pl.multiple_of(c * LANE, LANE)
                row = cum_ref[e:e + 1, pl.ds(start, LANE)]
                return acc + jnp.where(row <= slot, 1.0, 0.0)

            acc = lax.fori_loop(0, l // LANE, tile, jnp.zeros((jb, LANE), F32))
            tok = _dot_nt(ones, acc.astype(BF16))
            idx_ref[0, e:e + 1, j0:j0 + jb] = tok[0:1].astype(I32)


def _route(aff, cap):
    b, ne, l = aff.shape
    tri = jnp.asarray(np.triu(np.ones((LANE, LANE))), BF16)
    return pl.pallas_call(
        functools.partial(_route_kernel, cap=cap, jb=256),
        out_shape=jax.ShapeDtypeStruct((b, ne, cap), I32),
        grid=(b,),
        in_specs=[pl.BlockSpec((1, ne, l), lambda i: (i, 0, 0)),
                  pl.BlockSpec(tri.shape, lambda i: (0, 0))],
        out_specs=pl.BlockSpec((1, ne, cap), lambda i: (i, 0, 0)),
        scratch_shapes=[pltpu.VMEM((ne, l), F32)],
        compiler_params=pltpu.CompilerParams(
            dimension_semantics=("parallel",), vmem_limit_bytes=VMEM_LIMIT),
        name="route",
    )(aff, tri)


def _ffn_kernel(idx_ref, h_hbm, wg_ref, wu_ref, wd_ref, y_ref, xg_ref, xs_ref, sem, *, n_tokens):
    e = pl.program_id(0)
    b = pl.program_id(1)
    f = pl.program_id(2)
    cap = xs_ref.shape[0]

    def row_copy(j):
        t = idx_ref[(b * pl.num_programs(0) + e) * cap + j]
        return pltpu.make_async_copy(h_hbm.at[b * n_tokens + t], xg_ref.at[j], sem)

    @pl.when(f == 0)
    def _():
        def issue(j, carry):
            row_copy(j).start()
            return carry

        lax.fori_loop(0, cap, issue, 0)
        pltpu.make_async_copy(h_hbm.at[pl.ds(0, cap)], xg_ref, sem).wait()
        for s in range(xg_ref.shape[1]):
            xs_ref[:, s * LANE:(s + 1) * LANE] = xg_ref[:, s, :].astype(BF16)
        y_ref[0, 0] = jnp.zeros(y_ref.shape[2:], F32)

    xs = xs_ref[...]
    hid = _silu(_dot(xs, wg_ref[0].astype(BF16))) * _dot(xs, wu_ref[0].astype(BF16))
    y_ref[0, 0] += _dot(hid.astype(BF16), wd_ref[0].astype(BF16))


def _ffn(idx_flat, h_rows, w_gate, w_up, w_down, *, batch, cap, tf):
    ne, d, dff = w_gate.shape
    n_tokens = h_rows.shape[0] // batch
    return pl.pallas_call(
        functools.partial(_ffn_kernel, n_tokens=n_tokens),
        out_shape=jax.ShapeDtypeStruct((batch, ne, cap, d), F32),
        grid_spec=pltpu.PrefetchScalarGridSpec(
            num_scalar_prefetch=1,
            grid=(ne, batch, dff // tf),
            in_specs=[pl.BlockSpec(memory_space=pl.ANY),
                      pl.BlockSpec((1, d, tf), lambda e, b, f, idx: (e, 0, f)),
                      pl.BlockSpec((1, d, tf), lambda e, b, f, idx: (e, 0, f)),
                      pl.BlockSpec((1, tf, d), lambda e, b, f, idx: (e, f, 0))],
            out_specs=pl.BlockSpec((1, 1, cap, d), lambda e, b, f, idx: (b, e, 0, 0)),
            scratch_shapes=[pltpu.VMEM((cap, d // LANE, LANE), F32),
                            pltpu.VMEM((cap, d), BF16),
                            pltpu.SemaphoreType.DMA],
        ),
        compiler_params=pltpu.CompilerParams(
            dimension_semantics=("arbitrary", "arbitrary", "arbitrary"), vmem_limit_bytes=VMEM_LIMIT),
        name="ffn",
    )(idx_flat, h_rows, w_gate, w_up, w_down)


def _combine_kernel(idx_ref, aff_ref, y_ref, o_hbm, acc_ref, sem):
    b = pl.program_id(0)
    e = pl.program_id(1)
    ne = pl.num_programs(1)
    cap = y_ref.shape[2]

    @pl.when(e == 0)
    def _():
        acc_ref[...] = jnp.zeros(acc_ref.shape, F32)

    def add_row(j, carry):
        t = idx_ref[(b * ne + e) * cap + j]
        acc_ref[pl.ds(t, 1), :] += y_ref[0, 0, pl.ds(j, 1), :] * aff_ref[0, 0, t]
        return carry

    lax.fori_loop(0, cap, add_row, 0)

    @pl.when(e == ne - 1)
    def _():
        out = pltpu.make_async_copy(acc_ref, o_hbm.at[b], sem)
        out.start()
        out.wait()


def _combine(idx_flat, aff_rows, ye, n_tokens):
    b, ne, cap, d = ye.shape
    return pl.pallas_call(
        _combine_kernel,
        out_shape=jax.ShapeDtypeStruct((b, n_tokens, d), F32),
        grid_spec=pltpu.PrefetchScalarGridSpec(
            num_scalar_prefetch=1,
            grid=(b, ne),
            in_specs=[pl.BlockSpec((1, 1, n_tokens), lambda i, e, idx: (i * ne + e, 0, 0),
                                   memory_space=pltpu.SMEM),
                      pl.BlockSpec((1, 1, cap, d), lambda i, e, idx: (i, e, 0, 0))],
            out_specs=pl.BlockSpec(memory_space=pl.ANY),
            scratch_shapes=[pltpu.VMEM((n_tokens, d), F32), pltpu.SemaphoreType.DMA],
        ),
        compiler_params=pltpu.CompilerParams(
            dimension_semantics=("arbitrary", "arbitrary"), vmem_limit_bytes=VMEM_LIMIT),
        name="combine",
    )(idx_flat, aff_rows, ye)


def _final_kernel(x1_ref, moe_ref, mod_ref, g_ref, o_ref):
    d = x1_ref.shape[2]
    gate2 = mod_ref[pl.ds(pl.program_id(0), 1), 5 * d:6 * d]
    o_ref[0] = _rms(x1_ref[0] + gate2 * moe_ref[0]) * g_ref[...]


def _final(x1, moe, mod, g_final, tm):
    b, l, d = x1.shape
    tokens = pl.BlockSpec((1, tm, d), lambda i, j: (i, j, 0))
    return pl.pallas_call(
        _final_kernel,
        out_shape=jax.ShapeDtypeStruct((b, l, d), F32),
        grid=(b, l // tm),
        in_specs=[tokens, tokens, pl.BlockSpec(mod.shape, lambda i, j: (0, 0)),
                  pl.BlockSpec((1, d), lambda i, j: (0, 0))],
        out_specs=tokens,
        compiler_params=pltpu.CompilerParams(
            dimension_semantics=("parallel", "parallel"), vmem_limit_bytes=VMEM_LIMIT),
        name="final",
    )(x1, moe, mod, g_final.reshape(1, d))


def kernel(x, c, ctx, c_ctx, w_ada, b_ada, g_mix, w_in, w_fourier, lb_logits, g_rec, w_out, g_ffn, w_router, w_exp_gate, w_exp_up, w_exp_down, g_final):
    b, l, d = x.shape
    depth = w_ada.shape[0]
    assert depth == 1 and lb_logits.shape[0] == 2 and l == DFT_ROWS * DFT_COLS
    d_f = FOURIER_GROUPS * GROUP_DIM
    d_r = REC_HEADS * HEAD_DIM
    ne = w_router.shape[2]
    cap = EC_CAPACITY_FACTOR * l // ne

    cond = jnp.concatenate([c, c_ctx[None], jnp.zeros((8 - b - 1, d), F32)], axis=0)
    mod = _ada(cond, w_ada[0], b_ada[0])

    w_in_b = w_in[0].astype(BF16)
    col = lambda i: d_f + i * d_r
    lat_parts = (("raw", 0), ("silu", col(0)), ("raw", col(1)), ("logf_fwd", col(2)),
                 ("logf_bwd", col(3)), ("silu", col(4)))
    u, q, v, lcf, lcb, sg = _inproj(x, mod, g_mix[0], w_in_b, lb_logits, ctx_row=None, parts=lat_parts, tm=512)
    q_c, v_c, lcf_c, lcb_c = _inproj(ctx, mod, g_mix[0], w_in_b, lb_logits, ctx_row=b, parts=lat_parts[1:5],
                                     tm=ctx.shape[1])

    f1, twc, tws, f3a, f3b = _dft_tables()
    t4 = _dft1(u.reshape(b, DFT_ROWS, DFT_COLS, d_f), f1, twc, tws)
    xs = _dft3(t4, f3a, f3b).reshape(b, l, 2 * d_f)

    zero = jnp.zeros((b, REC_HEADS, HEAD_DIM, HEAD_DIM), F32)
    s_f, s_b = _rec(q_c, v_c, lcf_c, lcb_c, zero, zero, emit_out=False)
    o_f, o_b = _rec(q, v, lcf, lcb, s_f, s_b, emit_out=True)

    w_router_pad = jnp.pad(w_router[0], ((0, 0), (0, LANE - ne)))
    x1, h2, aff = _outproj(xs, o_f, o_b, sg, x, mod, g_rec[0], w_fourier[0].astype(BF16),
                           w_out[0].astype(BF16), g_ffn[0], w_router_pad, n_experts=ne, tm=512)

    idx_flat = _route(aff, cap).reshape(-1)
    ye = _ffn(idx_flat, h2.reshape(b * l, d // LANE, LANE), w_exp_gate[0], w_exp_up[0], w_exp_down[0],
              batch=b, cap=cap, tf=256)
    moe = _combine(idx_flat, aff.reshape(b * ne, 1, l), ye, l)
    return _final(x1, moe, mod, g_final, 512)
```

```python
import functools

import numpy as np
import jax
import jax.numpy as jnp
from jax import lax
from jax.experimental import pallas as pl
from jax.experimental.pallas import tpu as pltpu

F32 = jnp.float32
BF16 = jnp.bfloat16
I32 = jnp.int32

NORM_EPS = 1e-6
LANE = 128
N_ADA = 6
FOURIER_GROUPS = 4
GROUP_DIM = 128
REC_HEADS = 4
HEAD_DIM = 128
N_EXPERTS = 16
EC_CAPACITY_FACTOR = 2
DFT_ROWS = 128
DFT_COLS = 64
REC_CHUNK = 128
VMEM_LIMIT = 56 * 1024 * 1024


def _dot(a, b):
    return jnp.dot(a, b, preferred_element_type=F32)


def _dot_nt(a, b):
    return lax.dot_general(a, b, (((1,), (1,)), ((), ())), preferred_element_type=F32)


def _dot_tn(a, b):
    return lax.dot_general(a, b, (((0,), (0,)), ((), ())), preferred_element_type=F32)


def _split(x):
    hi = x.astype(BF16)
    lo = (x - hi.astype(F32)).astype(BF16)
    return hi, lo


def _dot3(a, b):
    a_hi, a_lo = _split(a)
    b_hi, b_lo = _split(b)
    return _dot(a_hi, b_hi) + _dot(a_lo, b_hi) + _dot(a_hi, b_lo)


def _silu(x):
    return x * jax.nn.sigmoid(x)


def _rms(x):
    return x * lax.rsqrt(jnp.mean(x * x, axis=-1, keepdims=True) + NORM_EPS)


def _bf16_table(a):
    return jnp.asarray(a, F32).astype(BF16)


def _ada_kernel(cond_ref, w_ref, b_ref, o_ref):
    o_ref[...] = _dot3(_silu(cond_ref[...]), w_ref[...]) + b_ref[...]


def _ada(cond, w, b):
    d = cond.shape[1]
    n = w.shape[1]
    return pl.pallas_call(
        _ada_kernel,
        out_shape=jax.ShapeDtypeStruct((cond.shape[0], n), F32),
        grid=(n // d,),
        in_specs=[pl.BlockSpec(cond.shape, lambda j: (0, 0)),
                  pl.BlockSpec((d, d), lambda j: (0, j)),
                  pl.BlockSpec((1, d), lambda j: (0, j))],
        out_specs=pl.BlockSpec((cond.shape[0], d), lambda j: (0, j)),
        compiler_params=pltpu.CompilerParams(vmem_limit_bytes=VMEM_LIMIT),
        name="ada",
    )(cond, w, b.reshape(1, n))


def _inproj_kernel(x_ref, mod_ref, gmix_ref, w_ref, lbl_ref, *out_refs, ctx_row, parts):
    d = x_ref.shape[2]
    row = ctx_row if ctx_row is not None else pl.program_id(0)
    shift = mod_ref[pl.ds(row, 1), 0:d]
    scale = mod_ref[pl.ds(row, 1), d:2 * d]
    h = _rms(x_ref[0]) * gmix_ref[...] * (1.0 + scale) + shift
    hb = h.astype(BF16)

    l0 = lbl_ref[0]
    l1 = lbl_ref[1]
    m = jnp.maximum(l0, l1)
    e0 = jnp.exp(l0 - m)
    e1 = jnp.exp(l1 - m)
    lb = e0 / (e0 + e1)

    for o_ref, (kind, col) in zip(out_refs, parts):
        w = o_ref.shape[2]
        p = _dot(hb, w_ref[:, col:col + w])
        if kind == "raw":
            o_ref[0] = p
        elif kind == "silu":
            o_ref[0] = _silu(p)
        else:
            lbd = lb[0:1] if kind == "logf_fwd" else lb[1:2]
            o_ref[0] = jnp.log(lbd + (1.0 - lbd) * jax.nn.sigmoid(p))


def _inproj(x, mod, g_mix, w_bf16, lb_logits, *, ctx_row, parts, tm):
    b, l, d = x.shape
    width = 512
    kern = functools.partial(_inproj_kernel, ctx_row=ctx_row, parts=parts)
    return pl.pallas_call(
        kern,
        out_shape=[jax.ShapeDtypeStruct((b, l, width), F32) for _ in parts],
        grid=(b, l // tm),
        in_specs=[pl.BlockSpec((1, tm, d), lambda i, j: (i, j, 0)),
                  pl.BlockSpec(mod.shape, lambda i, j: (0, 0)),
                  pl.BlockSpec((1, d), lambda i, j: (0, 0)),
                  pl.BlockSpec(w_bf16.shape, lambda i, j: (0, 0)),
                  pl.BlockSpec(lb_logits.shape, lambda i, j: (0, 0, 0))],
        out_specs=[pl.BlockSpec((1, tm, width), lambda i, j: (i, j, 0)) for _ in parts],
        compiler_params=pltpu.CompilerParams(
            dimension_semantics=("parallel", "parallel"), vmem_limit_bytes=VMEM_LIMIT),
        name="inproj_ctx" if ctx_row is not None else "inproj",
    )(x, mod, g_mix.reshape(1, d), w_bf16, lb_logits)


def _dft_tables():
    n1, n2 = DFT_ROWS, DFT_COLS
    k1 = np.arange(n1)
    a1 = 2.0 * np.pi * np.outer(k1, k1) / n1
    f1 = np.concatenate([np.cos(a1), -np.sin(a1)], axis=0)
    c = np.arange(n2)
    th = 2.0 * np.pi * np.outer(c, k1) / (n1 * n2)
    twc = np.repeat(np.cos(th)[:, :, None], LANE, axis=2)
    tws = np.repeat(np.sin(th)[:, :, None], LANE, axis=2)
    a2 = 2.0 * np.pi * np.outer(c, c) / n2
    f3a = np.concatenate([np.cos(a2), -np.sin(a2)], axis=0)
    f3b = np.concatenate([np.sin(a2), np.cos(a2)], axis=0)
    return (_bf16_table(f1), jnp.asarray(twc, F32), jnp.asarray(tws, F32),
            _bf16_table(f3a), _bf16_table(f3b))


def _dft1_kernel(x_ref, f1_ref, twc_ref, tws_ref, o_ref):
    n1 = x_ref.shape[1]
    ch = x_ref.shape[3]
    reps = ch // LANE
    for cc in range(x_ref.shape[2]):
        xc = x_ref[0, :, cc, :].astype(BF16)
        g = _dot(f1_ref[...], xc)
        tre = g[:n1]
        tim = g[n1:]
        cw = jnp.concatenate([twc_ref[cc]] * reps, axis=1)
        sw = jnp.concatenate([tws_ref[cc]] * reps, axis=1)
        o_ref[0, :, cc, 0:ch] = tre * cw + tim * sw
        o_ref[0, :, cc, ch:2 * ch] = tim * cw - tre * sw


def _dft1(u4, f1, twc, tws):
    b, n1, n2, ch = u4.shape
    cb = 8
    return pl.pallas_call(
        _dft1_kernel,
        out_shape=jax.ShapeDtypeStruct((b, n1, n2, 2 * ch), F32),
        grid=(b, n2 // cb),
        in_specs=[pl.BlockSpec((1, n1, cb, ch), lambda i, j: (i, 0, j, 0)),
                  pl.BlockSpec(f1.shape, lambda i, j: (0, 0)),
                  pl.BlockSpec((cb, n1, LANE), lambda i, j: (j, 0, 0)),
                  pl.BlockSpec((cb, n1, LANE), lambda i, j: (j, 0, 0))],
        out_specs=pl.BlockSpec((1, n1, cb, 2 * ch), lambda i, j: (i, 0, j, 0)),
        compiler_params=pltpu.CompilerParams(
            dimension_semantics=("parallel", "parallel"), vmem_limit_bytes=VMEM_LIMIT),
        name="dft1",
    )(u4, f1, twc, tws)


def _dft3_kernel(t_ref, f3a_ref, f3b_ref, o_ref):
    n2 = t_ref.shape[2]
    ch = t_ref.shape[3] // 2
    for kk in range(t_ref.shape[1]):
        t = t_ref[0, kk]
        x = _dot(f3a_ref[...], t[:, :ch].astype(BF16)) + _dot(f3b_ref[...], t[:, ch:].astype(BF16))
        o_ref[0, :, kk, 0:ch] = x[:n2]
        o_ref[0, :, kk, ch:2 * ch] = x[n2:]


def _dft3(t4, f3a, f3b):
    b, n1, n2, w = t4.shape
    kb = 8
    return pl.pallas_call(
        _dft3_kernel,
        out_shape=jax.ShapeDtypeStruct((b, n2, n1, w), F32),
        grid=(b, n1 // kb),
        in_specs=[pl.BlockSpec((1, kb, n2, w), lambda i, j: (i, j, 0, 0)),
                  pl.BlockSpec(f3a.shape, lambda i, j: (0, 0)),
                  pl.BlockSpec(f3b.shape, lambda i, j: (0, 0))],
        out_specs=pl.BlockSpec((1, n2, kb, w), lambda i, j: (i, 0, j, 0)),
        compiler_params=pltpu.CompilerParams(
            dimension_semantics=("parallel", "parallel"), vmem_limit_bytes=VMEM_LIMIT),
        name="dft3",
    )(t4, f3a, f3b)


def _rec_tables(reverse):
    c = REC_CHUNK
    t = np.arange(c)[:, None]
    s = np.arange(c)[None, :]
    masks = [np.eye(c)]
    m = 1
    while m < c:
        masks.append(((t // (2 * m) == s // (2 * m)) & ((t // m) % 2 == 1) & ((s // m) % 2 == 0)).astype(np.float64))
        m *= 2
    cum = (s <= t).astype(np.float64)
    if reverse:
        cum = cum[::-1, ::-1]
        masks = [a[::-1, ::-1] for a in masks]
    return jnp.asarray(cum, BF16), jnp.asarray(np.stack(masks), F32)


def _level_logdecay(b, lc, m, reverse):
    c, d = lc.shape
    if m <= 2:
        row = lax.broadcasted_iota(I32, (c, d), 0)
        if m == 1:
            return jnp.where((row & 1) == (0 if reverse else 1), lc, 0.0)
        r = row & 3
        nxt = pltpu.roll(lc, c - 1, 0)
        prv = pltpu.roll(lc, 1, 0)
        if reverse:
            return jnp.where(r == 0, lc + nxt, jnp.where(r == 1, lc, jnp.where(r == 2, 0.0, prv)))
        return jnp.where(r == 0, nxt, jnp.where(r == 1, 0.0, jnp.where(r == 2, lc, lc + prv)))
    anchor_row = m if reverse else m - 1
    b3 = b.reshape(c // (2 * m), 2 * m, d)
    anchor = jnp.broadcast_to(b3[:, anchor_row:anchor_row + 1, :], b3.shape).reshape(c, d)
    return -jnp.abs(b - anchor)


def _rec_direction(q, v, lc, st_ref, h, cum_ref, mask_ref, emit_out, reverse):
    c = REC_CHUNK
    k = 1.0 - jnp.exp(lc)
    hi, lo = _split(lc)
    res = _dot(cum_ref[...], jnp.concatenate([hi, lo], axis=1))
    b = res[:, :HEAD_DIM] + res[:, HEAD_DIM:]
    last = 0 if reverse else c - 1
    b_last = b[last:last + 1, :]
    st = st_ref[h]
    vb = v.astype(BF16)
    o = None
    if emit_out:
        a = mask_ref[0] * _dot_nt(q.astype(BF16), k.astype(BF16))
        m, i = 1, 1
        while m < c:
            dm = jnp.exp(_level_logdecay(b, lc, m, reverse))
            a = a + mask_ref[i] * _dot_nt((q * dm).astype(BF16), (k * dm).astype(BF16))
            m, i = 2 * m, i + 1
        qd = (q * jnp.exp(b)).astype(BF16)
        o = _dot_nt(qd, st.astype(BF16)) + _dot(a.astype(BF16), vb)
    kd = (k * jnp.exp(b_last - b)).astype(BF16)
    st_ref[h] = st * jnp.exp(b_last) + _dot_tn(vb, kd)
    return o


def _rec_kernel(qf_ref, vf_ref, lf_ref, qb_ref, vb_ref, lb_ref, s0f_ref, s0b_ref,
                wf_ref, mf_ref, wb_ref, mb_ref, of_ref, ob_ref, sf_ref, sb_ref, *, emit_out):
    n = pl.program_id(1)

    @pl.when(n == 0)
    def _():
        sf_ref[...] = s0f_ref[0]
        sb_ref[...] = s0b_ref[0]

    for h in range(REC_HEADS):
        cols = slice(h * HEAD_DIM, (h + 1) * HEAD_DIM)
        o_f = _rec_direction(qf_ref[0, :, cols], vf_ref[0, :, cols], lf_ref[0, :, cols],
                             sf_ref, h, wf_ref, mf_ref, emit_out, False)
        o_b = _rec_direction(qb_ref[0, :, cols], vb_ref[0, :, cols], lb_ref[0, :, cols],
                             sb_ref, h, wb_ref, mb_ref, emit_out, True)
        if emit_out:
            of_ref[0, :, cols] = o_f
            ob_ref[0, :, cols] = o_b
    if not emit_out:
        @pl.when(n == pl.num_programs(1) - 1)
        def _():
            of_ref[0] = sf_ref[...]
            ob_ref[0] = sb_ref[...]


def _rec(q, v, lcf, lcb, s0f, s0b, *, emit_out):
    b, l, w = q.shape
    c = REC_CHUNK
    n = l // c
    wf, mf = _rec_tables(False)
    wb, mb = _rec_tables(True)
    fwd = lambda i, j: (i, j, 0)
    bwd = lambda i, j: (i, n - 1 - j, 0)
    tok = lambda im: pl.BlockSpec((1, c, w), im)
    state = pl.BlockSpec((1, REC_HEADS, HEAD_DIM, HEAD_DIM), lambda i, j: (i, 0, 0, 0))
    const2 = lambda a: pl.BlockSpec(a.shape, lambda i, j: (0, 0))
    const3 = lambda a: pl.BlockSpec(a.shape, lambda i, j: (0, 0, 0))
    if emit_out:
        out_shape = [jax.ShapeDtypeStruct((b, l, w), F32)] * 2
        out_specs = [tok(fwd), tok(bwd)]
    else:
        out_shape = [jax.ShapeDtypeStruct((b, REC_HEADS, HEAD_DIM, HEAD_DIM), F32)] * 2
        out_specs = [state, state]
    state_scratch = pltpu.VMEM((REC_HEADS, HEAD_DIM, HEAD_DIM), F32)
    return pl.pallas_call(
        functools.partial(_rec_kernel, emit_out=emit_out),
        out_shape=out_shape,
        grid=(b, n),
        in_specs=[tok(fwd), tok(fwd), tok(fwd), tok(bwd), tok(bwd), tok(bwd), state, state,
                  const2(wf), const3(mf), const2(wb), const3(mb)],
        out_specs=out_specs,
        scratch_shapes=[state_scratch, state_scratch],
        compiler_params=pltpu.CompilerParams(
            dimension_semantics=("parallel", "arbitrary"), vmem_limit_bytes=VMEM_LIMIT),
        name="rec" if emit_out else "rec_ctx",
    )(q, v, lcf, q, v, lcb, s0f, s0b, wf, mf, wb, mb)


def _channel_dft_tables(scale):
    k = np.arange(GROUP_DIM)
    a = 2.0 * np.pi * np.outer(k, k) / GROUP_DIM
    return _bf16_table(np.cos(a) * scale), _bf16_table(np.sin(a) * scale)


def _outproj_kernel(xs_ref, of_ref, ob_ref, sg_ref, x_ref, mod_ref, grec_ref, cc_ref, sc_ref,
                    wf_ref, wo_ref, gffn_ref, wr_ref, x1_ref, h2_ref, aff_ref, *, n_experts):
    d = x_ref.shape[2]
    half = xs_ref.shape[2] // 2
    row = pl.program_id(0)
    gate1 = mod_ref[pl.ds(row, 1), 2 * d:3 * d]
    shift2 = mod_ref[pl.ds(row, 1), 3 * d:4 * d]
    scale2 = mod_ref[pl.ds(row, 1), 4 * d:5 * d]

    parts = []
    for g in range(FOURIER_GROUPS):
        lo = g * GROUP_DIM
        xre = xs_ref[0, :, lo:lo + GROUP_DIM].astype(BF16)
        xim = xs_ref[0, :, half + lo:half + lo + GROUP_DIM].astype(BF16)
        rez = _dot(xre, cc_ref[...]) + _dot(xim, sc_ref[...])
        parts.append(_dot(rez.astype(BF16), wf_ref[g]).astype(BF16))
    o = of_ref[0] + ob_ref[0]
    for h in range(REC_HEADS):
        lo = h * HEAD_DIM
        oh = _rms(o[:, lo:lo + HEAD_DIM]) * grec_ref[:, lo:lo + HEAD_DIM]
        parts.append((oh * sg_ref[0, :, lo:lo + HEAD_DIM]).astype(BF16))
    mix = _dot(jnp.concatenate(parts, axis=1), wo_ref[...])
    x1 = x_ref[0] + gate1 * mix
    x1_ref[0] = x1

    h2 = _rms(x1) * gffn_ref[...] * (1.0 + scale2) + shift2
    h2_ref[0] = h2
    logits = _dot3(h2, wr_ref[...])
    lane = lax.broadcasted_iota(I32, logits.shape, 1)
    logits = jnp.where(lane < n_experts, logits, -1e30)
    e = jnp.exp(logits - jnp.max(logits, axis=-1, keepdims=True))
    aff = e / jnp.sum(e, axis=-1, keepdims=True)
    aff_ref[0] = aff.T[:n_experts]


def _outproj(xs, o_f, o_b, sg, x, mod, g_rec, w_fourier_bf16, w_out_bf16, g_ffn, w_router_pad, *, n_experts, tm):
    b, l, d = x.shape
    cc, sc = _channel_dft_tables(1.0 / np.sqrt(l * GROUP_DIM))
    tokens = lambda w: pl.BlockSpec((1, tm, w), lambda i, j: (i, j, 0))
    full = lambda a: pl.BlockSpec(a.shape, lambda i, j: (0,) * a.ndim)
    g_rec2 = g_rec.reshape(1, -1)
    g_ffn2 = g_ffn.reshape(1, d)
    return pl.pallas_call(
        functools.partial(_outproj_kernel, n_experts=n_experts),
        out_shape=[jax.ShapeDtypeStruct((b, l, d), F32),
                   jax.ShapeDtypeStruct((b, l, d), F32),
                   jax.ShapeDtypeStruct((b, n_experts, l), F32)],
        grid=(b, l // tm),
        in_specs=[tokens(xs.shape[2]), tokens(o_f.shape[2]), tokens(o_b.shape[2]), tokens(sg.shape[2]),
                  tokens(d), full(mod), full(g_rec2), full(cc), full(sc), full(w_fourier_bf16),
                  full(w_out_bf16), full(g_ffn2), full(w_router_pad)],
        out_specs=[tokens(d),
                   tokens(d),
                   pl.BlockSpec((1, n_experts, tm), lambda i, j: (i, 0, j))],
        compiler_params=pltpu.CompilerParams(
            dimension_semantics=("parallel", "parallel"), vmem_limit_bytes=VMEM_LIMIT),
        name="outproj",
    )(xs, o_f, o_b, sg, x, mod, g_rec2, cc, sc, w_fourier_bf16, w_out_bf16, g_ffn2, w_router_pad)


def _route_kernel(aff_ref, tri_ref, idx_ref, cum_ref, *, cap, jb):
    a = aff_ref[0]
    ne, l = a.shape

    def value_bit(i, thr):
        cand = thr | jnp.left_shift(jnp.int32(1), 30 - i)
        cnt = jnp.sum(jnp.where(a >= pltpu.bitcast(cand, F32), 1.0, 0.0), axis=1, keepdims=True)
        return jnp.where(cnt >= cap, cand, thr)

    thr = lax.fori_loop(0, 31, value_bit, jnp.zeros((ne, 1), I32))
    gt = a >= pltpu.bitcast(thr + 1, F32)
    eq = (a >= pltpu.bitcast(thr, F32)) & jnp.logical_not(gt)
    need = cap - jnp.sum(jnp.where(gt, 1.0, 0.0), axis=1, keepdims=True)
    pos = lax.broadcasted_iota(I32, (ne, l), 1)
    nbits = int(l).bit_length()

    def index_bit(i, x):
        cand = x + jnp.left_shift(jnp.int32(1), nbits - 1 - i)
        cnt = jnp.sum(jnp.where(eq & (pos < cand), 1.0, 0.0), axis=1, keepdims=True)
        return jnp.where(cnt < need, cand, x)

    last = lax.fori_loop(0, nbits, index_bit, jnp.zeros((ne, 1), I32))
    sel = (gt | (eq & (pos <= last))).astype(BF16)

    off = jnp.zeros((ne, 1), F32)
    for c in range(l // LANE):
        p = _dot(sel[:, c * LANE:(c + 1) * LANE], tri_ref[...]) + off
        cum_ref[:, c * LANE:(c + 1) * LANE] = p
        off = p[:, LANE - 1:LANE]

    ones = jnp.ones((8, LANE), BF16)
    for e in range(ne):
        for j0 in range(0, cap, jb):
            slot = (lax.broadcasted_iota(I32, (jb, LANE), 0) + j0).astype(F32)

            def tile(c, acc):
                start = pl.multiple_of(c * LANE, LANE)
                row = cum_ref[e:e + 1, pl.ds(start, LANE)]
                return acc + jnp.where(row <= slot, 1.0, 0.0)

            acc = lax.fori_loop(0, l // LANE, tile, jnp.zeros((jb, LANE), F32))
            tok = _dot_nt(ones, acc.astype(BF16))
            idx_ref[0, e:e + 1, j0:j0 + jb] = tok[0:1].astype(I32)


def _route(aff, cap):
    b, ne, l = aff.shape
    tri = jnp.asarray(np.triu(np.ones((LANE, LANE))), BF16)
    return pl.pallas_call(
        functools.partial(_route_kernel, cap=cap, jb=256),
        out_shape=jax.ShapeDtypeStruct((b, ne, cap), I32),
        grid=(b,),
        in_specs=[pl.BlockSpec((1, ne, l), lambda i: (i, 0, 0)),
                  pl.BlockSpec(tri.shape, lambda i: (0, 0))],
        out_specs=pl.BlockSpec((1, ne, cap), lambda i: (i, 0, 0)),
        scratch_shapes=[pltpu.VMEM((ne, l), F32)],
        compiler_params=pltpu.CompilerParams(
            dimension_semantics=("parallel",), vmem_limit_bytes=VMEM_LIMIT),
        name="route",
    )(aff, tri)


def _ffn_kernel(idx_ref, h_hbm, wg_ref, wu_ref, wd_ref, y_ref, xg_ref, xs_ref, sem, *,
                n_tokens, ne, nb, nf, rows_per_step):
    e = pl.program_id(0)
    b = pl.program_id(1)
    f = pl.program_id(2)
    cap = xs_ref.shape[0]
    n_rows = rows_per_step * nf
    n_groups = ne * nb
    g = e * nb + b
    slot = g % 2

    def start_row(group, slot_, j):
        ge = group // nb
        gb = group % nb
        t = idx_ref[(gb * ne + ge) * cap + jnp.minimum(j, cap - 1)]
        pltpu.make_async_copy(h_hbm.at[pl.ds(gb * n_tokens + t, 1)],
                              xg_ref.at[slot_, pl.ds(j, 1)], sem.at[slot_]).start()

    def wait_rows(slot_):
        pltpu.make_async_copy(h_hbm.at[pl.ds(0, n_rows)],
                              xg_ref.at[slot_, pl.ds(0, n_rows)], sem.at[slot_]).wait()

    @pl.when((g == 0) & (f == 0))
    def _():
        def issue(j, carry):
            start_row(0, 0, j)
            return carry

        lax.fori_loop(0, n_rows, issue, 0)

    @pl.when(f == 0)
    def _():
        wait_rows(slot)
        xs_ref[...] = xg_ref[slot, 0:cap, :].astype(BF16)
        y_ref[0, 0] = jnp.zeros(y_ref.shape[2:], F32)

    nxt = jnp.minimum(g + 1, n_groups - 1)
    for i in range(rows_per_step):
        start_row(nxt, 1 - slot, f * rows_per_step + i)

    xs = xs_ref[...]
    hid = _silu(_dot(xs, wg_ref[0].astype(BF16))) * _dot(xs, wu_ref[0].astype(BF16))
    y_ref[0, 0] += _dot(hid.astype(BF16), wd_ref[0].astype(BF16))

    @pl.when((g == n_groups - 1) & (f == nf - 1))
    def _():
        wait_rows(1 - slot)


def _ffn(idx_flat, h_rows, w_gate, w_up, w_down, *, batch, cap, tf):
    ne, d, dff = w_gate.shape
    n_tokens = h_rows.shape[0] // batch
    nf = dff // tf
    rows_per_step = -(-cap // (8 * nf)) * 8
    pad_rows = rows_per_step * nf
    return pl.pallas_call(
        functools.partial(_ffn_kernel, n_tokens=n_tokens, ne=ne, nb=batch, nf=nf, rows_per_step=rows_per_step),
        out_shape=jax.ShapeDtypeStruct((batch, ne, cap, d), F32),
        grid_spec=pltpu.PrefetchScalarGridSpec(
            num_scalar_prefetch=1,
            grid=(ne, batch, nf),
            in_specs=[pl.BlockSpec(memory_space=pl.ANY),
                      pl.BlockSpec((1, d, tf), lambda e, b, f, idx: (e, 0, f)),
                      pl.BlockSpec((1, d, tf), lambda e, b, f, idx: (e, 0, f)),
                      pl.BlockSpec((1, tf, d), lambda e, b, f, idx: (e, f, 0))],
            out_specs=pl.BlockSpec((1, 1, cap, d), lambda e, b, f, idx: (b, e, 0, 0)),
            scratch_shapes=[pltpu.VMEM((2, pad_rows, d), F32),
                            pltpu.VMEM((cap, d), BF16),
                            pltpu.SemaphoreType.DMA((2,))],
        ),
        compiler_params=pltpu.CompilerParams(
            dimension_semantics=("arbitrary", "arbitrary", "arbitrary"), vmem_limit_bytes=VMEM_LIMIT),
        name="ffn",
    )(idx_flat, h_rows, w_gate, w_up, w_down)


def _combine_kernel(idx_ref, aff_ref, y_ref, o_hbm, acc_ref, sem):
    b = pl.program_id(0)
    e = pl.program_id(1)
    ne = pl.num_programs(1)
    cap = y_ref.shape[2]

    @pl.when(e == 0)
    def _():
        acc_ref[...] = jnp.zeros(acc_ref.shape, F32)

    def add_row(j, carry):
        t = idx_ref[(b * ne + e) * cap + j]
        acc_ref[pl.ds(t, 1), :] += y_ref[0, 0, pl.ds(j, 1), :] * aff_ref[0, 0, t]
        return carry

    lax.fori_loop(0, cap, add_row, 0)

    @pl.when(e == ne - 1)
    def _():
        out = pltpu.make_async_copy(acc_ref, o_hbm.at[b], sem)
        out.start()
        out.wait()


def _combine(idx_flat, aff_rows, ye, n_tokens):
    b, ne, cap, d = ye.shape
    return pl.pallas_call(
        _combine_kernel,
        out_shape=jax.ShapeDtypeStruct((b, n_tokens, d), F32),
        grid_spec=pltpu.PrefetchScalarGridSpec(
            num_scalar_prefetch=1,
            grid=(b, ne),
            in_specs=[pl.BlockSpec((1, 1, n_tokens), lambda i, e, idx: (i * ne + e, 0, 0),
                                   memory_space=pltpu.SMEM),
                      pl.BlockSpec((1, 1, cap, d), lambda i, e, idx: (i, e, 0, 0))],
            out_specs=pl.BlockSpec(memory_space=pl.ANY),
            scratch_shapes=[pltpu.VMEM((n_tokens, d), F32), pltpu.SemaphoreType.DMA],
        ),
        compiler_params=pltpu.CompilerParams(
            dimension_semantics=("arbitrary", "arbitrary"), vmem_limit_bytes=VMEM_LIMIT),
        name="combine",
    )(idx_flat, aff_rows, ye)


def _final_kernel(x1_ref, moe_ref, mod_ref, g_ref, o_ref):
    d = x1_ref.shape[2]
    gate2 = mod_ref[pl.ds(pl.program_id(0), 1), 5 * d:6 * d]
    o_ref[0] = _rms(x1_ref[0] + gate2 * moe_ref[0]) * g_ref[...]


def _final(x1, moe, mod, g_final, tm):
    b, l, d = x1.shape
    tokens = pl.BlockSpec((1, tm, d), lambda i, j: (i, j, 0))
    return pl.pallas_call(
        _final_kernel,
        out_shape=jax.ShapeDtypeStruct((b, l, d), F32),
        grid=(b, l // tm),
        in_specs=[tokens, tokens, pl.BlockSpec(mod.shape, lambda i, j: (0, 0)),
                  pl.BlockSpec((1, d), lambda i, j: (0, 0))],
        out_specs=tokens,
        compiler_params=pltpu.CompilerParams(
            dimension_semantics=("parallel", "parallel"), vmem_limit_bytes=VMEM_LIMIT),
        name="final",
    )(x1, moe, mod, g_final.reshape(1, d))


def kernel(x, c, ctx, c_ctx, w_ada, b_ada, g_mix, w_in, w_fourier, lb_logits, g_rec, w_out, g_ffn, w_router, w_exp_gate, w_exp_up, w_exp_down, g_final):
    b, l, d = x.shape
    depth = w_ada.shape[0]
    assert depth == 1 and lb_logits.shape[0] == 2 and l == DFT_ROWS * DFT_COLS
    d_f = FOURIER_GROUPS * GROUP_DIM
    d_r = REC_HEADS * HEAD_DIM
    ne = w_router.shape[2]
    cap = EC_CAPACITY_FACTOR * l // ne

    cond = jnp.concatenate([c, c_ctx[None], jnp.zeros((8 - b - 1, d), F32)], axis=0)
    mod = _ada(cond, w_ada[0], b_ada[0])

    w_in_b = w_in[0].astype(BF16)
    col = lambda i: d_f + i * d_r
    lat_parts = (("raw", 0), ("silu", col(0)), ("raw", col(1)), ("logf_fwd", col(2)),
                 ("logf_bwd", col(3)), ("silu", col(4)))
    u, q, v, lcf, lcb, sg = _inproj(x, mod, g_mix[0], w_in_b, lb_logits, ctx_row=None, parts=lat_parts, tm=512)
    q_c, v_c, lcf_c, lcb_c = _inproj(ctx, mod, g_mix[0], w_in_b, lb_logits, ctx_row=b, parts=lat_parts[1:5],
                                     tm=ctx.shape[1])

    f1, twc, tws, f3a, f3b = _dft_tables()
    t4 = _dft1(u.reshape(b, DFT_ROWS, DFT_COLS, d_f), f1, twc, tws)
    xs = _dft3(t4, f3a, f3b).reshape(b, l, 2 * d_f)

    zero = jnp.zeros((b, REC_HEADS, HEAD_DIM, HEAD_DIM), F32)
    s_f, s_b = _rec(q_c, v_c, lcf_c, lcb_c, zero, zero, emit_out=False)
    o_f, o_b = _rec(q, v, lcf, lcb, s_f, s_b, emit_out=True)

    w_router_pad = jnp.pad(w_router[0], ((0, 0), (0, LANE - ne)))
    x1, h2, aff = _outproj(xs, o_f, o_b, sg, x, mod, g_rec[0], w_fourier[0].astype(BF16),
                           w_out[0].astype(BF16), g_ffn[0], w_router_pad, n_experts=ne, tm=512)

    idx_flat = _route(aff, cap).reshape(-1)
    ye = _ffn(idx_flat, h2.reshape(b * l, d), w_exp_gate[0], w_exp_up[0], w_exp_down[0],
              batch=b, cap=cap, tf=256)
    moe = _combine(idx_flat, aff.reshape(b * ne, 1, l), ye, l)
    return _final(x1, moe, mod, g_final, 512)
```

```python
import functools

import numpy as np
import jax
import jax.numpy as jnp
from jax import lax
from jax.experimental import pallas as pl
from jax.experimental.pallas import tpu as pltpu

F32 = jnp.float32
BF16 = jnp.bfloat16
I32 = jnp.int32

NORM_EPS = 1e-6
LANE = 128
N_ADA = 6
FOURIER_GROUPS = 4
GROUP_DIM = 128
REC_HEADS = 4
HEAD_DIM = 128
N_EXPERTS = 16
EC_CAPACITY_FACTOR = 2
DFT_ROWS = 128
DFT_COLS = 64
REC_CHUNK = 128
VMEM_LIMIT = 56 * 1024 * 1024


def _dot(a, b):
    return jnp.dot(a, b, preferred_element_type=F32)


def _dot_nt(a, b):
    return lax.dot_general(a, b, (((1,), (1,)), ((), ())), preferred_element_type=F32)


def _dot_tn(a, b):
    return lax.dot_general(a, b, (((0,), (0,)), ((), ())), preferred_element_type=F32)


def _split(x):
    hi = x.astype(BF16)
    lo = (x - hi.astype(F32)).astype(BF16)
    return hi, lo


def _dot3(a, b):
    a_hi, a_lo = _split(a)
    b_hi, b_lo = _split(b)
    return _dot(a_hi, b_hi) + _dot(a_lo, b_hi) + _dot(a_hi, b_lo)


def _silu(x):
    return x * jax.nn.sigmoid(x)


def _rms(x):
    return x * lax.rsqrt(jnp.mean(x * x, axis=-1, keepdims=True) + NORM_EPS)


def _bf16_table(a):
    return jnp.asarray(a, F32).astype(BF16)


def _ada_kernel(cond_ref, w_ref, b_ref, o_ref):
    o_ref[...] = _dot3(_silu(cond_ref[...]), w_ref[...]) + b_ref[...]


def _ada(cond, w, b):
    d = cond.shape[1]
    n = w.shape[1]
    return pl.pallas_call(
        _ada_kernel,
        out_shape=jax.ShapeDtypeStruct((cond.shape[0], n), F32),
        grid=(n // d,),
        in_specs=[pl.BlockSpec(cond.shape, lambda j: (0, 0)),
                  pl.BlockSpec((d, d), lambda j: (0, j)),
                  pl.BlockSpec((1, d), lambda j: (0, j))],
        out_specs=pl.BlockSpec((cond.shape[0], d), lambda j: (0, j)),
        compiler_params=pltpu.CompilerParams(vmem_limit_bytes=VMEM_LIMIT),
        name="ada",
    )(cond, w, b.reshape(1, n))


def _inproj_kernel(x_ref, mod_ref, gmix_ref, w_ref, lbl_ref, *out_refs, ctx_row, parts):
    d = x_ref.shape[2]
    row = ctx_row if ctx_row is not None else pl.program_id(0)
    shift = mod_ref[pl.ds(row, 1), 0:d]
    scale = mod_ref[pl.ds(row, 1), d:2 * d]
    h = _rms(x_ref[0]) * gmix_ref[...] * (1.0 + scale) + shift
    hb = h.astype(BF16)

    l0 = lbl_ref[0]
    l1 = lbl_ref[1]
    m = jnp.maximum(l0, l1)
    e0 = jnp.exp(l0 - m)
    e1 = jnp.exp(l1 - m)
    lb = e0 / (e0 + e1)

    for o_ref, (kind, col) in zip(out_refs, parts):
        w = o_ref.shape[2]
        p = _dot(hb, w_ref[:, col:col + w])
        if kind == "raw":
            o_ref[0] = p
        elif kind == "silu":
            o_ref[0] = _silu(p)
        else:
            lbd = lb[0:1] if kind == "logf_fwd" else lb[1:2]
            o_ref[0] = jnp.log(lbd + (1.0 - lbd) * jax.nn.sigmoid(p))


def _inproj(x, mod, g_mix, w_bf16, lb_logits, *, ctx_row, parts, tm):
    b, l, d = x.shape
    width = 512
    kern = functools.partial(_inproj_kernel, ctx_row=ctx_row, parts=parts)
    return pl.pallas_call(
        kern,
        out_shape=[jax.ShapeDtypeStruct((b, l, width), F32) for _ in parts],
        grid=(b, l // tm),
        in_specs=[pl.BlockSpec((1, tm, d), lambda i, j: (i, j, 0)),
                  pl.BlockSpec(mod.shape, lambda i, j: (0, 0)),
                  pl.BlockSpec((1, d), lambda i, j: (0, 0)),
                  pl.BlockSpec(w_bf16.shape, lambda i, j: (0, 0)),
                  pl.BlockSpec(lb_logits.shape, lambda i, j: (0, 0, 0))],
        out_specs=[pl.BlockSpec((1, tm, width), lambda i, j: (i, j, 0)) for _ in parts],
        compiler_params=pltpu.CompilerParams(
            dimension_semantics=("parallel", "parallel"), vmem_limit_bytes=VMEM_LIMIT),
        name="inproj_ctx" if ctx_row is not None else "inproj",
    )(x, mod, g_mix.reshape(1, d), w_bf16, lb_logits)


def _dft_tables():
    n1, n2 = DFT_ROWS, DFT_COLS
    k1 = np.arange(n1)
    a1 = 2.0 * np.pi * np.outer(k1, k1) / n1
    f1 = np.concatenate([np.cos(a1), -np.sin(a1)], axis=0)
    c = np.arange(n2)
    th = 2.0 * np.pi * np.outer(c, k1) / (n1 * n2)
    twc = np.repeat(np.cos(th)[:, :, None], LANE, axis=2)
    tws = np.repeat(np.sin(th)[:, :, None], LANE, axis=2)
    a2 = 2.0 * np.pi * np.outer(c, c) / n2
    f3a = np.concatenate([np.cos(a2), -np.sin(a2)], axis=0)
    f3b = np.concatenate([np.sin(a2), np.cos(a2)], axis=0)
    return (_bf16_table(f1), jnp.asarray(twc, F32), jnp.asarray(tws, F32),
            _bf16_table(f3a), _bf16_table(f3b))


def _dft1_kernel(x_ref, f1_ref, twc_ref, tws_ref, o_ref):
    n1 = x_ref.shape[1]
    ch = x_ref.shape[3]
    reps = ch // LANE
    for cc in range(x_ref.shape[2]):
        xc = x_ref[0, :, cc, :].astype(BF16)
        g = _dot(f1_ref[...], xc)
        tre = g[:n1]
        tim = g[n1:]
        cw = jnp.concatenate([twc_ref[cc]] * reps, axis=1)
        sw = jnp.concatenate([tws_ref[cc]] * reps, axis=1)
        o_ref[0, :, cc, 0:ch] = tre * cw + tim * sw
        o_ref[0, :, cc, ch:2 * ch] = tim * cw - tre * sw


def _dft1(u4, f1, twc, tws):
    b, n1, n2, ch = u4.shape
    cb = 8
    return pl.pallas_call(
        _dft1_kernel,
        out_shape=jax.ShapeDtypeStruct((b, n1, n2, 2 * ch), F32),
        grid=(b, n2 // cb),
        in_specs=[pl.BlockSpec((1, n1, cb, ch), lambda i, j: (i, 0, j, 0)),
                  pl.BlockSpec(f1.shape, lambda i, j: (0, 0)),
                  pl.BlockSpec((cb, n1, LANE), lambda i, j: (j, 0, 0)),
                  pl.BlockSpec((cb, n1, LANE), lambda i, j: (j, 0, 0))],
        out_specs=pl.BlockSpec((1, n1, cb, 2 * ch), lambda i, j: (i, 0, j, 0)),
        compiler_params=pltpu.CompilerParams(
            dimension_semantics=("parallel", "parallel"), vmem_limit_bytes=VMEM_LIMIT),
        name="dft1",
    )(u4, f1, twc, tws)


def _dft3_kernel(t_ref, f3a_ref, f3b_ref, o_ref):
    n2 = t_ref.shape[2]
    ch = t_ref.shape[3] // 2
    for kk in range(t_ref.shape[1]):
        t = t_ref[0, kk]
        x = _dot(f3a_ref[...], t[:, :ch].astype(BF16)) + _dot(f3b_ref[...], t[:, ch:].astype(BF16))
        o_ref[0, :, kk, 0:ch] = x[:n2]
        o_ref[0, :, kk, ch:2 * ch] = x[n2:]


def _dft3(t4, f3a, f3b):
    b, n1, n2, w = t4.shape
    kb = 8
    return pl.pallas_call(
        _dft3_kernel,
        out_shape=jax.ShapeDtypeStruct((b, n2, n1, w), F32),
        grid=(b, n1 // kb),
        in_specs=[pl.BlockSpec((1, kb, n2, w), lambda i, j: (i, j, 0, 0)),
                  pl.BlockSpec(f3a.shape, lambda i, j: (0, 0)),
                  pl.BlockSpec(f3b.shape, lambda i, j: (0, 0))],
        out_specs=pl.BlockSpec((1, n2, kb, w), lambda i, j: (i, 0, j, 0)),
        compiler_params=pltpu.CompilerParams(
            dimension_semantics=("parallel", "parallel"), vmem_limit_bytes=VMEM_LIMIT),
        name="dft3",
    )(t4, f3a, f3b)


def _rec_tables(reverse):
    c = REC_CHUNK
    t = np.arange(c)[:, None]
    s = np.arange(c)[None, :]
    masks = [np.eye(c)]
    m = 1
    while m < c:
        masks.append(((t // (2 * m) == s // (2 * m)) & ((t // m) % 2 == 1) & ((s // m) % 2 == 0)).astype(np.float64))
        m *= 2
    cum = (s <= t).astype(np.float64)
    if reverse:
        cum = cum[::-1, ::-1]
        masks = [a[::-1, ::-1] for a in masks]
    return jnp.asarray(cum, BF16), jnp.asarray(np.stack(masks), F32)


def _level_logdecay(b, lc, m, reverse):
    c, d = lc.shape
    if m <= 2:
        row = lax.broadcasted_iota(I32, (c, d), 0)
        if m == 1:
            return jnp.where((row & 1) == (0 if reverse else 1), lc, 0.0)
        r = row & 3
        nxt = pltpu.roll(lc, c - 1, 0)
        prv = pltpu.roll(lc, 1, 0)
        if reverse:
            return jnp.where(r == 0, lc + nxt, jnp.where(r == 1, lc, jnp.where(r == 2, 0.0, prv)))
        return jnp.where(r == 0, nxt, jnp.where(r == 1, 0.0, jnp.where(r == 2, lc, lc + prv)))
    anchor_row = m if reverse else m - 1
    b3 = b.reshape(c // (2 * m), 2 * m, d)
    anchor = jnp.broadcast_to(b3[:, anchor_row:anchor_row + 1, :], b3.shape).reshape(c, d)
    return -jnp.abs(b - anchor)


def _rec_direction(q, v, lc, st_ref, h, cum_ref, mask_ref, emit_out, reverse):
    c = REC_CHUNK
    k = 1.0 - jnp.exp(lc)
    hi, lo = _split(lc)
    res = _dot(cum_ref[...], jnp.concatenate([hi, lo], axis=1))
    b = res[:, :HEAD_DIM] + res[:, HEAD_DIM:]
    last = 0 if reverse else c - 1
    b_last = b[last:last + 1, :]
    st = st_ref[h]
    vb = v.astype(BF16)
    o = None
    if emit_out:
        a = mask_ref[0] * _dot_nt(q.astype(BF16), k.astype(BF16))
        m, i = 1, 1
        while m < c:
            dm = jnp.exp(_level_logdecay(b, lc, m, reverse))
            a = a + mask_ref[i] * _dot_nt((q * dm).astype(BF16), (k * dm).astype(BF16))
            m, i = 2 * m, i + 1
        qd = (q * jnp.exp(b)).astype(BF16)
        o = _dot_nt(qd, st.astype(BF16)) + _dot(a.astype(BF16), vb)
    kd = (k * jnp.exp(b_last - b)).astype(BF16)
    st_ref[h] = st * jnp.exp(b_last) + _dot_tn(vb, kd)
    return o


def _rec_kernel(qf_ref, vf_ref, lf_ref, qb_ref, vb_ref, lb_ref, s0f_ref, s0b_ref,
                wf_ref, mf_ref, wb_ref, mb_ref, of_ref, ob_ref, sf_ref, sb_ref, *, emit_out):
    n = pl.program_id(1)

    @pl.when(n == 0)
    def _():
        sf_ref[...] = s0f_ref[0]
        sb_ref[...] = s0b_ref[0]

    for h in range(REC_HEADS):
        cols = slice(h * HEAD_DIM, (h + 1) * HEAD_DIM)
        o_f = _rec_direction(qf_ref[0, :, cols], vf_ref[0, :, cols], lf_ref[0, :, cols],
                             sf_ref, h, wf_ref, mf_ref, emit_out, False)
        o_b = _rec_direction(qb_ref[0, :, cols], vb_ref[0, :, cols], lb_ref[0, :, cols],
                             sb_ref, h, wb_ref, mb_ref, emit_out, True)
        if emit_out:
            of_ref[0, :, cols] = o_f
            ob_ref[0, :, cols] = o_b
    if not emit_out:
        @pl.when(n == pl.num_programs(1) - 1)
        def _():
            of_ref[0] = sf_ref[...]
            ob_ref[0] = sb_ref[...]


def _rec(q, v, lcf, lcb, s0f, s0b, *, emit_out):
    b, l, w = q.shape
    c = REC_CHUNK
    n = l // c
    wf, mf = _rec_tables(False)
    wb, mb = _rec_tables(True)
    fwd = lambda i, j: (i, j, 0)
    bwd = lambda i, j: (i, n - 1 - j, 0)
    tok = lambda im: pl.BlockSpec((1, c, w), im)
    state = pl.BlockSpec((1, REC_HEADS, HEAD_DIM, HEAD_DIM), lambda i, j: (i, 0, 0, 0))
    const2 = lambda a: pl.BlockSpec(a.shape, lambda i, j: (0, 0))
    const3 = lambda a: pl.BlockSpec(a.shape, lambda i, j: (0, 0, 0))
    if emit_out:
        out_shape = [jax.ShapeDtypeStruct((b, l, w), F32)] * 2
        out_specs = [tok(fwd), tok(bwd)]
    else:
        out_shape = [jax.ShapeDtypeStruct((b, REC_HEADS, HEAD_DIM, HEAD_DIM), F32)] * 2
        out_specs = [state, state]
    state_scratch = pltpu.VMEM((REC_HEADS, HEAD_DIM, HEAD_DIM), F32)
    return pl.pallas_call(
        functools.partial(_rec_kernel, emit_out=emit_out),
        out_shape=out_shape,
        grid=(b, n),
        in_specs=[tok(fwd), tok(fwd), tok(fwd), tok(bwd), tok(bwd), tok(bwd), state, state,
                  const2(wf), const3(mf), const2(wb), const3(mb)],
        out_specs=out_specs,
        scratch_shapes=[state_scratch, state_scratch],
        compiler_params=pltpu.CompilerParams(
            dimension_semantics=("parallel", "arbitrary"), vmem_limit_bytes=VMEM_LIMIT),
        name="rec" if emit_out else "rec_ctx",
    )(q, v, lcf, q, v, lcb, s0f, s0b, wf, mf, wb, mb)


def _channel_dft_tables(scale):
    k = np.arange(GROUP_DIM)
    a = 2.0 * np.pi * np.outer(k, k) / GROUP_DIM
    return _bf16_table(np.cos(a) * scale), _bf16_table(np.sin(a) * scale)


def _outproj_kernel(xs_ref, of_ref, ob_ref, sg_ref, x_ref, mod_ref, grec_ref, cc_ref, sc_ref,
                    wf_ref, wo_ref, gffn_ref, wr_ref, x1_ref, h2_ref, aff_ref, *, n_experts):
    d = x_ref.shape[2]
    half = xs_ref.shape[2] // 2
    row = pl.program_id(0)
    gate1 = mod_ref[pl.ds(row, 1), 2 * d:3 * d]
    shift2 = mod_ref[pl.ds(row, 1), 3 * d:4 * d]
    scale2 = mod_ref[pl.ds(row, 1), 4 * d:5 * d]

    parts = []
    for g in range(FOURIER_GROUPS):
        lo = g * GROUP_DIM
        xre = xs_ref[0, :, lo:lo + GROUP_DIM].astype(BF16)
        xim = xs_ref[0, :, half + lo:half + lo + GROUP_DIM].astype(BF16)
        rez = _dot(xre, cc_ref[...]) + _dot(xim, sc_ref[...])
        parts.append(_dot(rez.astype(BF16), wf_ref[g]).astype(BF16))
    o = of_ref[0] + ob_ref[0]
    for h in range(REC_HEADS):
        lo = h * HEAD_DIM
        oh = _rms(o[:, lo:lo + HEAD_DIM]) * grec_ref[:, lo:lo + HEAD_DIM]
        parts.append((oh * sg_ref[0, :, lo:lo + HEAD_DIM]).astype(BF16))
    mix = _dot(jnp.concatenate(parts, axis=1), wo_ref[...])
    x1 = x_ref[0] + gate1 * mix
    x1_ref[0] = x1

    h2 = _rms(x1) * gffn_ref[...] * (1.0 + scale2) + shift2
    h2_ref[0] = h2
    logits = _dot3(h2, wr_ref[...])
    lane = lax.broadcasted_iota(I32, logits.shape, 1)
    logits = jnp.where(lane < n_experts, logits, -1e30)
    e = jnp.exp(logits - jnp.max(logits, axis=-1, keepdims=True))
    aff = e / jnp.sum(e, axis=-1, keepdims=True)
    aff_ref[0] = aff.T[:n_experts]


def _outproj(xs, o_f, o_b, sg, x, mod, g_rec, w_fourier_bf16, w_out_bf16, g_ffn, w_router_pad, *, n_experts, tm):
    b, l, d = x.shape
    cc, sc = _channel_dft_tables(1.0 / np.sqrt(l * GROUP_DIM))
    tokens = lambda w: pl.BlockSpec((1, tm, w), lambda i, j: (i, j, 0))
    full = lambda a: pl.BlockSpec(a.shape, lambda i, j: (0,) * a.ndim)
    g_rec2 = g_rec.reshape(1, -1)
    g_ffn2 = g_ffn.reshape(1, d)
    return pl.pallas_call(
        functools.partial(_outproj_kernel, n_experts=n_experts),
        out_shape=[jax.ShapeDtypeStruct((b, l, d), F32),
                   jax.ShapeDtypeStruct((b, l, d), F32),
                   jax.ShapeDtypeStruct((b, n_experts, l), F32)],
        grid=(b, l // tm),
        in_specs=[tokens(xs.shape[2]), tokens(o_f.shape[2]), tokens(o_b.shape[2]), tokens(sg.shape[2]),
                  tokens(d), full(mod), full(g_rec2), full(cc), full(sc), full(w_fourier_bf16),
                  full(w_out_bf16), full(g_ffn2), full(w_router_pad)],
        out_specs=[tokens(d),
                   tokens(d),
                   pl.BlockSpec((1, n_experts, tm), lambda i, j: (i, 0, j))],
        compiler_params=pltpu.CompilerParams(
            dimension_semantics=("parallel", "parallel"), vmem_limit_bytes=VMEM_LIMIT),
        name="outproj",
    )(xs, o_f, o_b, sg, x, mod, g_rec2, cc, sc, w_fourier_bf16, w_out_bf16, g_ffn2, w_router_pad)


def _route_kernel(aff_ref, tri_ref, idx_ref, cum_ref, *, cap):
    a = aff_ref[0]
    ne, l = a.shape

    def value_bit(i, thr):
        cand = thr | jnp.left_shift(jnp.int32(1), 30 - i)
        cnt = jnp.sum(jnp.where(a >= pltpu.bitcast(cand, F32), 1.0, 0.0), axis=1, keepdims=True)
        return jnp.where(cnt >= cap, cand, thr)

    thr = lax.fori_loop(0, 31, value_bit, jnp.zeros((ne, 1), I32))
    gt = a >= pltpu.bitcast(thr + 1, F32)
    eq = (a >= pltpu.bitcast(thr, F32)) & jnp.logical_not(gt)
    need = cap - jnp.sum(jnp.where(gt, 1.0, 0.0), axis=1, keepdims=True)
    pos = lax.broadcasted_iota(I32, (ne, l), 1)
    nbits = int(l).bit_length()

    def index_bit(i, x):
        cand = x + jnp.left_shift(jnp.int32(1), nbits - 1 - i)
        cnt = jnp.sum(jnp.where(eq & (pos < cand), 1.0, 0.0), axis=1, keepdims=True)
        return jnp.where(cnt < need, cand, x)

    last = lax.fori_loop(0, nbits, index_bit, jnp.zeros((ne, 1), I32))
    sel = (gt | (eq & (pos <= last))).astype(BF16)

    n_tiles = l // LANE
    cum_ref[:, n_tiles:, :] = jnp.zeros((ne, LANE - n_tiles, LANE), F32)
    lane = lax.broadcasted_iota(I32, (ne, LANE), 1)
    off = jnp.zeros((ne, 1), F32)
    ends = jnp.zeros((ne, LANE), F32)
    for c in range(n_tiles):
        p = _dot(sel[:, c * LANE:(c + 1) * LANE], tri_ref[...]) + off
        cum_ref[:, c, :] = p
        off = p[:, LANE - 1:LANE]
        ends = jnp.where(lane == c, off, ends)
    never = float(2 * l)
    ends = jnp.where(lane < n_tiles, ends, never)
    starts = jnp.where(lane == 0, 0.0, pltpu.roll(ends, 1, 1))

    ones = jnp.ones((8, LANE), BF16)
    slot = lax.broadcasted_iota(I32, (cap, LANE), 0).astype(F32)
    for e in range(ne):
        en = ends[e:e + 1, :]
        st = starts[e:e + 1, :]
        before = en <= slot
        holds = ((st <= slot) & (slot < en)).astype(BF16)
        hi, lo = _split(cum_ref[e])
        tile_cum = _dot(holds, hi) + _dot(holds, lo)
        counts = jnp.where(before, float(LANE), 0.0) + jnp.where(tile_cum <= slot, 1.0, 0.0)
        tok = _dot_nt(ones, counts.astype(BF16))
        idx_ref[0, e:e + 1, :] = tok[0:1].astype(I32)


def _route(aff, cap):
    b, ne, l = aff.shape
    assert l % LANE == 0 and l // LANE <= LANE
    tri = jnp.asarray(np.triu(np.ones((LANE, LANE))), BF16)
    return pl.pallas_call(
        functools.partial(_route_kernel, cap=cap),
        out_shape=jax.ShapeDtypeStruct((b, ne, cap), I32),
        grid=(b,),
        in_specs=[pl.BlockSpec((1, ne, l), lambda i: (i, 0, 0)),
                  pl.BlockSpec(tri.shape, lambda i: (0, 0))],
        out_specs=pl.BlockSpec((1, ne, cap), lambda i: (i, 0, 0)),
        scratch_shapes=[pltpu.VMEM((ne, LANE, LANE), F32)],
        compiler_params=pltpu.CompilerParams(
            dimension_semantics=("parallel",), vmem_limit_bytes=VMEM_LIMIT),
        name="route",
    )(aff, tri)


def _ffn_kernel(idx_ref, h_hbm, wg_ref, wu_ref, wd_ref, y_ref, xg_ref, xs_ref, sem, *,
                n_tokens, ne, nb, nf, rows_per_step):
    e = pl.program_id(0)
    f = pl.program_id(1)
    cap = xs_ref.shape[0] // nb
    n_rows = rows_per_step * nf
    slot = e % 2

    def start_row(expert, slot_, r):
        rr = jnp.minimum(r, nb * cap - 1)
        gb = rr // cap
        t = idx_ref[(gb * ne + expert) * cap + rr % cap]
        pltpu.make_async_copy(h_hbm.at[pl.ds(gb * n_tokens + t, 1)],
                              xg_ref.at[slot_, pl.ds(r, 1)], sem.at[slot_]).start()

    def wait_rows(slot_):
        pltpu.make_async_copy(h_hbm.at[pl.ds(0, n_rows)],
                              xg_ref.at[slot_, pl.ds(0, n_rows)], sem.at[slot_]).wait()

    @pl.when((e == 0) & (f == 0))
    def _():
        def issue(r, carry):
            start_row(0, 0, r)
            return carry

        lax.fori_loop(0, n_rows, issue, 0)

    @pl.when(f == 0)
    def _():
        wait_rows(slot)
        xs_ref[...] = xg_ref[slot, 0:nb * cap, :].astype(BF16)
        y_ref[...] = jnp.zeros(y_ref.shape, F32)

    nxt = jnp.minimum(e + 1, ne - 1)
    for i in range(rows_per_step):
        start_row(nxt, 1 - slot, f * rows_per_step + i)

    wg = wg_ref[0].astype(BF16)
    wu = wu_ref[0].astype(BF16)
    wd = wd_ref[0].astype(BF16)
    for gb in range(nb):
        xs = xs_ref[gb * cap:(gb + 1) * cap, :]
        hid = _silu(_dot(xs, wg)) * _dot(xs, wu)
        y_ref[gb, 0] += _dot(hid.astype(BF16), wd)

    @pl.when((e == ne - 1) & (f == nf - 1))
    def _():
        wait_rows(1 - slot)


def _ffn(idx_flat, h_rows, w_gate, w_up, w_down, *, batch, cap, tf):
    ne, d, dff = w_gate.shape
    n_tokens = h_rows.shape[0] // batch
    nf = dff // tf
    rows_per_step = -(-(batch * cap) // (8 * nf)) * 8
    pad_rows = rows_per_step * nf
    return pl.pallas_call(
        functools.partial(_ffn_kernel, n_tokens=n_tokens, ne=ne, nb=batch, nf=nf, rows_per_step=rows_per_step),
        out_shape=jax.ShapeDtypeStruct((batch, ne, cap, d), F32),
        grid_spec=pltpu.PrefetchScalarGridSpec(
            num_scalar_prefetch=1,
            grid=(ne, nf),
            in_specs=[pl.BlockSpec(memory_space=pl.ANY),
                      pl.BlockSpec((1, d, tf), lambda e, f, idx: (e, 0, f)),
                      pl.BlockSpec((1, d, tf), lambda e, f, idx: (e, 0, f)),
                      pl.BlockSpec((1, tf, d), lambda e, f, idx: (e, f, 0))],
            out_specs=pl.BlockSpec((batch, 1, cap, d), lambda e, f, idx: (0, e, 0, 0)),
            scratch_shapes=[pltpu.VMEM((2, pad_rows, d), F32),
                            pltpu.VMEM((batch * cap, d), BF16),
                            pltpu.SemaphoreType.DMA((2,))],
        ),
        compiler_params=pltpu.CompilerParams(
            dimension_semantics=("arbitrary", "arbitrary"), vmem_limit_bytes=VMEM_LIMIT),
        name="ffn",
    )(idx_flat, h_rows, w_gate, w_up, w_down)


def _combine_kernel(idx_ref, aff_ref, y_ref, o_hbm, acc_ref, sem):
    b = pl.program_id(0)
    e = pl.program_id(1)
    ne = pl.num_programs(1)
    cap = y_ref.shape[2]

    @pl.when(e == 0)
    def _():
        acc_ref[...] = jnp.zeros(acc_ref.shape, F32)

    def add_row(j, carry):
        t = idx_ref[(b * ne + e) * cap + j]
        acc_ref[pl.ds(t, 1), :] += y_ref[0, 0, pl.ds(j, 1), :] * aff_ref[0, 0, t]
        return carry

    lax.fori_loop(0, cap, add_row, 0, unroll=8)

    @pl.when(e == ne - 1)
    def _():
        out = pltpu.make_async_copy(acc_ref, o_hbm.at[b], sem)
        out.start()
        out.wait()


def _combine(idx_flat, aff_rows, ye, n_tokens):
    b, ne, cap, d = ye.shape
    return pl.pallas_call(
        _combine_kernel,
        out_shape=jax.ShapeDtypeStruct((b, n_tokens, d), F32),
        grid_spec=pltpu.PrefetchScalarGridSpec(
            num_scalar_prefetch=1,
            grid=(b, ne),
            in_specs=[pl.BlockSpec((1, 1, n_tokens), lambda i, e, idx: (i * ne + e, 0, 0),
                                   memory_space=pltpu.SMEM),
                      pl.BlockSpec((1, 1, cap, d), lambda i, e, idx: (i, e, 0, 0))],
            out_specs=pl.BlockSpec(memory_space=pl.ANY),
            scratch_shapes=[pltpu.VMEM((n_tokens, d), F32), pltpu.SemaphoreType.DMA],
        ),
        compiler_params=pltpu.CompilerParams(
            dimension_semantics=("arbitrary", "arbitrary"), vmem_limit_bytes=VMEM_LIMIT),
        name="combine",
    )(idx_flat, aff_rows, ye)


def _final_kernel(x1_ref, moe_ref, mod_ref, g_ref, o_ref):
    d = x1_ref.shape[2]
    gate2 = mod_ref[pl.ds(pl.program_id(0), 1), 5 * d:6 * d]
    o_ref[0] = _rms(x1_ref[0] + gate2 * moe_ref[0]) * g_ref[...]


def _final(x1, moe, mod, g_final, tm):
    b, l, d = x1.shape
    tokens = pl.BlockSpec((1, tm, d), lambda i, j: (i, j, 0))
    return pl.pallas_call(
        _final_kernel,
        out_shape=jax.ShapeDtypeStruct((b, l, d), F32),
        grid=(b, l // tm),
        in_specs=[tokens, tokens, pl.BlockSpec(mod.shape, lambda i, j: (0, 0)),
                  pl.BlockSpec((1, d), lambda i, j: (0, 0))],
        out_specs=tokens,
        compiler_params=pltpu.CompilerParams(
            dimension_semantics=("parallel", "parallel"), vmem_limit_bytes=VMEM_LIMIT),
        name="final",
    )(x1, moe, mod, g_final.reshape(1, d))


def kernel(x, c, ctx, c_ctx, w_ada, b_ada, g_mix, w_in, w_fourier, lb_logits, g_rec, w_out, g_ffn, w_router, w_exp_gate, w_exp_up, w_exp_down, g_final):
    b, l, d = x.shape
    depth = w_ada.shape[0]
    assert depth == 1 and lb_logits.shape[0] == 2 and l == DFT_ROWS * DFT_COLS
    d_f = FOURIER_GROUPS * GROUP_DIM
    d_r = REC_HEADS * HEAD_DIM
    ne = w_router.shape[2]
    cap = EC_CAPACITY_FACTOR * l // ne

    cond = jnp.concatenate([c, c_ctx[None], jnp.zeros((8 - b - 1, d), F32)], axis=0)
    mod = _ada(cond, w_ada[0], b_ada[0])

    w_in_b = w_in[0].astype(BF16)
    col = lambda i: d_f + i * d_r
    lat_parts = (("raw", 0), ("silu", col(0)), ("raw", col(1)), ("logf_fwd", col(2)),
                 ("logf_bwd", col(3)), ("silu", col(4)))
    u, q, v, lcf, lcb, sg = _inproj(x, mod, g_mix[0], w_in_b, lb_logits, ctx_row=None, parts=lat_parts, tm=512)
    q_c, v_c, lcf_c, lcb_c = _inproj(ctx, mod, g_mix[0], w_in_b, lb_logits, ctx_row=b, parts=lat_parts[1:5],
                                     tm=ctx.shape[1])

    f1, twc, tws, f3a, f3b = _dft_tables()
    t4 = _dft1(u.reshape(b, DFT_ROWS, DFT_COLS, d_f), f1, twc, tws)
    xs = _dft3(t4, f3a, f3b).reshape(b, l, 2 * d_f)

    zero = jnp.zeros((b, REC_HEADS, HEAD_DIM, HEAD_DIM), F32)
    s_f, s_b = _rec(q_c, v_c, lcf_c, lcb_c, zero, zero, emit_out=False)
    o_f, o_b = _rec(q, v, lcf, lcb, s_f, s_b, emit_out=True)

    w_router_pad = jnp.pad(w_router[0], ((0, 0), (0, LANE - ne)))
    x1, h2, aff = _outproj(xs, o_f, o_b, sg, x, mod, g_rec[0], w_fourier[0].astype(BF16),
                           w_out[0].astype(BF16), g_ffn[0], w_router_pad, n_experts=ne, tm=512)

    idx_flat = _route(aff, cap).reshape(-1)
    ye = _ffn(idx_flat, h2.reshape(b * l, d), w_exp_gate[0], w_exp_up[0], w_exp_down[0],
              batch=b, cap=cap, tf=256)
    moe = _combine(idx_flat, aff.reshape(b * ne, 1, l), ye, l)
    return _final(x1, moe, mod, g_final, 512)
```

```python
import functools

import numpy as np
import jax
import jax.numpy as jnp
from jax import lax
from jax.experimental import pallas as pl
from jax.experimental.pallas import tpu as pltpu

F32 = jnp.float32
BF16 = jnp.bfloat16
I32 = jnp.int32

NORM_EPS = 1e-6
LOG2_E = 1.4426950408889634
LANE = 128
N_ADA = 6
FOURIER_GROUPS = 4
GROUP_DIM = 128
REC_HEADS = 4
HEAD_DIM = 128
N_EXPERTS = 16
EC_CAPACITY_FACTOR = 2
DFT_ROWS = 128
DFT_COLS = 64
REC_CHUNK = 128
VMEM_LIMIT = 56 * 1024 * 1024


def _dot(a, b):
    return jnp.dot(a, b, preferred_element_type=F32)


def _dot_nt(a, b):
    return lax.dot_general(a, b, (((1,), (1,)), ((), ())), preferred_element_type=F32)


def _dot_tn(a, b):
    return lax.dot_general(a, b, (((0,), (0,)), ((), ())), preferred_element_type=F32)


def _split(x):
    hi = x.astype(BF16)
    lo = (x - hi.astype(F32)).astype(BF16)
    return hi, lo


def _dot3(a, b):
    a_hi, a_lo = _split(a)
    b_hi, b_lo = _split(b)
    return _dot(a_hi, b_hi) + _dot(a_lo, b_hi) + _dot(a_hi, b_lo)


def _silu(x):
    return x * jax.nn.sigmoid(x)


def _rms(x):
    return x * lax.rsqrt(jnp.mean(x * x, axis=-1, keepdims=True) + NORM_EPS)


def _bf16_table(a):
    return jnp.asarray(a, F32).astype(BF16)


def _ada_kernel(cond_ref, w_ref, b_ref, o_ref):
    o_ref[...] = _dot3(_silu(cond_ref[...]), w_ref[...]) + b_ref[...]


def _ada(cond, w, b):
    d = cond.shape[1]
    n = w.shape[1]
    return pl.pallas_call(
        _ada_kernel,
        out_shape=jax.ShapeDtypeStruct((cond.shape[0], n), F32),
        grid=(n // d,),
        in_specs=[pl.BlockSpec(cond.shape, lambda j: (0, 0)),
                  pl.BlockSpec((d, d), lambda j: (0, j)),
                  pl.BlockSpec((1, d), lambda j: (0, j))],
        out_specs=pl.BlockSpec((cond.shape[0], d), lambda j: (0, j)),
        compiler_params=pltpu.CompilerParams(vmem_limit_bytes=VMEM_LIMIT),
        name="ada",
    )(cond, w, b.reshape(1, n))


def _inproj_kernel(x_ref, mod_ref, gmix_ref, w_ref, lbl_ref, *out_refs, ctx_row, parts):
    d = x_ref.shape[2]
    row = ctx_row if ctx_row is not None else pl.program_id(0)
    shift = mod_ref[pl.ds(row, 1), 0:d]
    scale = mod_ref[pl.ds(row, 1), d:2 * d]
    h = _rms(x_ref[0]) * gmix_ref[...] * (1.0 + scale) + shift
    hb = h.astype(BF16)

    l0 = lbl_ref[0]
    l1 = lbl_ref[1]
    m = jnp.maximum(l0, l1)
    e0 = jnp.exp(l0 - m)
    e1 = jnp.exp(l1 - m)
    lb = e0 / (e0 + e1)

    for o_ref, (kind, col, _) in zip(out_refs, parts):
        w = o_ref.shape[2]
        p = _dot(hb, w_ref[:, col:col + w])
        if kind == "raw":
            o_ref[0] = p.astype(o_ref.dtype)
        elif kind == "silu":
            o_ref[0] = _silu(p).astype(o_ref.dtype)
        else:
            lbd = lb[0:1] if kind == "logf_fwd" else lb[1:2]
            o_ref[0] = jnp.log(lbd + (1.0 - lbd) * jax.nn.sigmoid(p))


def _inproj(x, mod, g_mix, w_bf16, lb_logits, *, ctx_row, parts, tm):
    b, l, d = x.shape
    width = 512
    kern = functools.partial(_inproj_kernel, ctx_row=ctx_row, parts=parts)
    return pl.pallas_call(
        kern,
        out_shape=[jax.ShapeDtypeStruct((b, l, width), dt) for _, _, dt in parts],
        grid=(b, l // tm),
        in_specs=[pl.BlockSpec((1, tm, d), lambda i, j: (i, j, 0)),
                  pl.BlockSpec(mod.shape, lambda i, j: (0, 0)),
                  pl.BlockSpec((1, d), lambda i, j: (0, 0)),
                  pl.BlockSpec(w_bf16.shape, lambda i, j: (0, 0)),
                  pl.BlockSpec(lb_logits.shape, lambda i, j: (0, 0, 0))],
        out_specs=[pl.BlockSpec((1, tm, width), lambda i, j: (i, j, 0)) for _ in parts],
        compiler_params=pltpu.CompilerParams(
            dimension_semantics=("parallel", "parallel"), vmem_limit_bytes=VMEM_LIMIT),
        name="inproj_ctx" if ctx_row is not None else "inproj",
    )(x, mod, g_mix.reshape(1, d), w_bf16, lb_logits)


def _dft_tables():
    n1, n2 = DFT_ROWS, DFT_COLS
    k1 = np.arange(n1)
    a1 = 2.0 * np.pi * np.outer(k1, k1) / n1
    f1 = np.concatenate([np.cos(a1), -np.sin(a1)], axis=0)
    c = np.arange(n2)
    th = 2.0 * np.pi * np.outer(c, k1) / (n1 * n2)
    twc = np.repeat(np.cos(th)[:, :, None], LANE, axis=2)
    tws = np.repeat(np.sin(th)[:, :, None], LANE, axis=2)
    a2 = 2.0 * np.pi * np.outer(c, c) / n2
    f3a = np.concatenate([np.cos(a2), -np.sin(a2)], axis=0)
    f3b = np.concatenate([np.sin(a2), np.cos(a2)], axis=0)
    return (_bf16_table(f1), jnp.asarray(twc, F32), jnp.asarray(tws, F32),
            _bf16_table(f3a), _bf16_table(f3b))


def _dft1_kernel(x_ref, f1_ref, twc_ref, tws_ref, o_ref):
    n1 = x_ref.shape[1]
    ch = x_ref.shape[3]
    reps = ch // LANE
    for cc in range(x_ref.shape[2]):
        xc = x_ref[0, :, cc, :].astype(BF16)
        g = _dot(f1_ref[...], xc)
        tre = g[:n1]
        tim = g[n1:]
        cw = jnp.concatenate([twc_ref[cc]] * reps, axis=1)
        sw = jnp.concatenate([tws_ref[cc]] * reps, axis=1)
        o_ref[0, :, cc, 0:ch] = tre * cw + tim * sw
        o_ref[0, :, cc, ch:2 * ch] = tim * cw - tre * sw


def _dft1(u4, f1, twc, tws):
    b, n1, n2, ch = u4.shape
    cb = 8
    return pl.pallas_call(
        _dft1_kernel,
        out_shape=jax.ShapeDtypeStruct((b, n1, n2, 2 * ch), F32),
        grid=(b, n2 // cb),
        in_specs=[pl.BlockSpec((1, n1, cb, ch), lambda i, j: (i, 0, j, 0)),
                  pl.BlockSpec(f1.shape, lambda i, j: (0, 0)),
                  pl.BlockSpec((cb, n1, LANE), lambda i, j: (j, 0, 0)),
                  pl.BlockSpec((cb, n1, LANE), lambda i, j: (j, 0, 0))],
        out_specs=pl.BlockSpec((1, n1, cb, 2 * ch), lambda i, j: (i, 0, j, 0)),
        compiler_params=pltpu.CompilerParams(
            dimension_semantics=("parallel", "parallel"), vmem_limit_bytes=VMEM_LIMIT),
        name="dft1",
    )(u4, f1, twc, tws)


def _dft3_kernel(t_ref, f3a_ref, f3b_ref, o_ref):
    n2 = t_ref.shape[2]
    ch = t_ref.shape[3] // 2
    for kk in range(t_ref.shape[1]):
        t = t_ref[0, kk]
        x = _dot(f3a_ref[...], t[:, :ch].astype(BF16)) + _dot(f3b_ref[...], t[:, ch:].astype(BF16))
        o_ref[0, :, kk, 0:ch] = x[:n2]
        o_ref[0, :, kk, ch:2 * ch] = x[n2:]


def _dft3(t4, f3a, f3b):
    b, n1, n2, w = t4.shape
    kb = 8
    return pl.pallas_call(
        _dft3_kernel,
        out_shape=jax.ShapeDtypeStruct((b, n2, n1, w), F32),
        grid=(b, n1 // kb),
        in_specs=[pl.BlockSpec((1, kb, n2, w), lambda i, j: (i, j, 0, 0)),
                  pl.BlockSpec(f3a.shape, lambda i, j: (0, 0)),
                  pl.BlockSpec(f3b.shape, lambda i, j: (0, 0))],
        out_specs=pl.BlockSpec((1, n2, kb, w), lambda i, j: (i, 0, j, 0)),
        compiler_params=pltpu.CompilerParams(
            dimension_semantics=("parallel", "parallel"), vmem_limit_bytes=VMEM_LIMIT),
        name="dft3",
    )(t4, f3a, f3b)


def _rec_tables(reverse):
    c = REC_CHUNK
    t = np.arange(c)[:, None]
    s = np.arange(c)[None, :]
    masks = [np.eye(c)]
    m = 1
    while m < c:
        masks.append(((t // (2 * m) == s // (2 * m)) & ((t // m) % 2 == 1) & ((s // m) % 2 == 0)).astype(np.float64))
        m *= 2
    cum = (s <= t).astype(np.float64)
    if reverse:
        cum = cum[::-1, ::-1]
        masks = [a[::-1, ::-1] for a in masks]
    return jnp.asarray(cum, BF16), jnp.asarray(np.stack(masks), F32)


def _level_logdecay(b, lc, m, reverse):
    c, d = lc.shape
    if m <= 2:
        row = lax.broadcasted_iota(I32, (c, d), 0)
        if m == 1:
            return jnp.where((row & 1) == (0 if reverse else 1), lc, 0.0)
        r = row & 3
        nxt = pltpu.roll(lc, c - 1, 0)
        prv = pltpu.roll(lc, 1, 0)
        if reverse:
            return jnp.where(r == 0, lc + nxt, jnp.where(r == 1, lc, jnp.where(r == 2, 0.0, prv)))
        return jnp.where(r == 0, nxt, jnp.where(r == 1, 0.0, jnp.where(r == 2, lc, lc + prv)))
    anchor_row = m if reverse else m - 1
    b3 = b.reshape(c // (2 * m), 2 * m, d)
    anchor = jnp.broadcast_to(b3[:, anchor_row:anchor_row + 1, :], b3.shape).reshape(c, d)
    diff = b - anchor
    return jnp.minimum(diff, -diff)


def _rec_direction(q, v, lc, st_ref, h, cum_ref, mask_ref, emit_out, reverse):
    c = REC_CHUNK
    lc = lc * LOG2_E
    k = 1.0 - jnp.exp2(lc)
    hi, lo = _split(lc)
    res = _dot(cum_ref[...], jnp.concatenate([hi, lo], axis=1))
    b = res[:, :HEAD_DIM] + res[:, HEAD_DIM:]
    last = 0 if reverse else c - 1
    b_last = b[last:last + 1, :]
    st = st_ref[h]
    kb = k.astype(BF16)
    o = None
    if emit_out:
        a = mask_ref[0] * _dot_nt(q, kb)
        m, i = 1, 1
        while m < c:
            dm = jnp.exp2(_level_logdecay(b, lc, m, reverse)).astype(BF16)
            a = a + mask_ref[i] * _dot_nt(q * dm, kb * dm)
            m, i = 2 * m, i + 1
        qd = q * jnp.exp2(b).astype(BF16)
        o = _dot_nt(qd, st.astype(BF16)) + _dot(a.astype(BF16), v)
    kd = kb * jnp.exp2(b_last - b).astype(BF16)
    st_ref[h] = st * jnp.exp2(b_last) + _dot_tn(v, kd)
    return o


def _rec_kernel(qf_ref, vf_ref, lf_ref, qb_ref, vb_ref, lb_ref, s0f_ref, s0b_ref,
                wf_ref, mf_ref, wb_ref, mb_ref, of_ref, ob_ref, sf_ref, sb_ref, *, emit_out):
    n = pl.program_id(1)

    @pl.when(n == 0)
    def _():
        sf_ref[...] = s0f_ref[0]
        sb_ref[...] = s0b_ref[0]

    for h in range(REC_HEADS):
        cols = slice(h * HEAD_DIM, (h + 1) * HEAD_DIM)
        o_f = _rec_direction(qf_ref[0, :, cols], vf_ref[0, :, cols], lf_ref[0, :, cols],
                             sf_ref, h, wf_ref, mf_ref, emit_out, False)
        o_b = _rec_direction(qb_ref[0, :, cols], vb_ref[0, :, cols], lb_ref[0, :, cols],
                             sb_ref, h, wb_ref, mb_ref, emit_out, True)
        if emit_out:
            of_ref[0, :, cols] = o_f.astype(of_ref.dtype)
            ob_ref[0, :, cols] = o_b.astype(ob_ref.dtype)
    if not emit_out:
        @pl.when(n == pl.num_programs(1) - 1)
        def _():
            of_ref[0] = sf_ref[...]
            ob_ref[0] = sb_ref[...]


def _rec(q, v, lcf, lcb, s0f, s0b, *, emit_out):
    b, l, w = q.shape
    c = REC_CHUNK
    n = l // c
    wf, mf = _rec_tables(False)
    wb, mb = _rec_tables(True)
    fwd = lambda i, j: (i, j, 0)
    bwd = lambda i, j: (i, n - 1 - j, 0)
    tok = lambda im: pl.BlockSpec((1, c, w), im)
    state = pl.BlockSpec((1, REC_HEADS, HEAD_DIM, HEAD_DIM), lambda i, j: (i, 0, 0, 0))
    const2 = lambda a: pl.BlockSpec(a.shape, lambda i, j: (0, 0))
    const3 = lambda a: pl.BlockSpec(a.shape, lambda i, j: (0, 0, 0))
    if emit_out:
        out_shape = [jax.ShapeDtypeStruct((b, l, w), BF16)] * 2
        out_specs = [tok(fwd), tok(bwd)]
    else:
        out_shape = [jax.ShapeDtypeStruct((b, REC_HEADS, HEAD_DIM, HEAD_DIM), F32)] * 2
        out_specs = [state, state]
    state_scratch = pltpu.VMEM((REC_HEADS, HEAD_DIM, HEAD_DIM), F32)
    return pl.pallas_call(
        functools.partial(_rec_kernel, emit_out=emit_out),
        out_shape=out_shape,
        grid=(b, n),
        in_specs=[tok(fwd), tok(fwd), tok(fwd), tok(bwd), tok(bwd), tok(bwd), state, state,
                  const2(wf), const3(mf), const2(wb), const3(mb)],
        out_specs=out_specs,
        scratch_shapes=[state_scratch, state_scratch],
        compiler_params=pltpu.CompilerParams(
            dimension_semantics=("parallel", "arbitrary"), vmem_limit_bytes=VMEM_LIMIT),
        name="rec" if emit_out else "rec_ctx",
    )(q, v, lcf, q, v, lcb, s0f, s0b, wf, mf, wb, mb)


def _channel_dft_tables(scale):
    k = np.arange(GROUP_DIM)
    a = 2.0 * np.pi * np.outer(k, k) / GROUP_DIM
    return _bf16_table(np.cos(a) * scale), _bf16_table(np.sin(a) * scale)


def _outproj_kernel(xs_ref, of_ref, ob_ref, sg_ref, x_ref, mod_ref, grec_ref, cc_ref, sc_ref,
                    wf_ref, wo_ref, gffn_ref, wr_ref, x1_ref, h2_ref, aff_ref, *, n_experts):
    d = x_ref.shape[2]
    half = xs_ref.shape[2] // 2
    row = pl.program_id(0)
    gate1 = mod_ref[pl.ds(row, 1), 2 * d:3 * d]
    shift2 = mod_ref[pl.ds(row, 1), 3 * d:4 * d]
    scale2 = mod_ref[pl.ds(row, 1), 4 * d:5 * d]

    parts = []
    for g in range(FOURIER_GROUPS):
        lo = g * GROUP_DIM
        xre = xs_ref[0, :, lo:lo + GROUP_DIM].astype(BF16)
        xim = xs_ref[0, :, half + lo:half + lo + GROUP_DIM].astype(BF16)
        rez = _dot(xre, cc_ref[...]) + _dot(xim, sc_ref[...])
        parts.append(_dot(rez.astype(BF16), wf_ref[g]).astype(BF16))
    o = of_ref[0].astype(F32) + ob_ref[0].astype(F32)
    for h in range(REC_HEADS):
        lo = h * HEAD_DIM
        oh = _rms(o[:, lo:lo + HEAD_DIM]) * grec_ref[:, lo:lo + HEAD_DIM]
        parts.append((oh * sg_ref[0, :, lo:lo + HEAD_DIM]).astype(BF16))
    mix = _dot(jnp.concatenate(parts, axis=1), wo_ref[...])
    x1 = x_ref[0] + gate1 * mix
    x1_ref[0] = x1

    h2 = _rms(x1) * gffn_ref[...] * (1.0 + scale2) + shift2
    h2_ref[0] = h2
    logits = _dot3(h2, wr_ref[...])
    lane = lax.broadcasted_iota(I32, logits.shape, 1)
    logits = jnp.where(lane < n_experts, logits, -1e30)
    e = jnp.exp(logits - jnp.max(logits, axis=-1, keepdims=True))
    aff = e / jnp.sum(e, axis=-1, keepdims=True)
    aff_ref[0] = aff.T[:n_experts]


def _outproj(xs, o_f, o_b, sg, x, mod, g_rec, w_fourier_bf16, w_out_bf16, g_ffn, w_router_pad, *, n_experts, tm):
    b, l, d = x.shape
    cc, sc = _channel_dft_tables(1.0 / np.sqrt(l * GROUP_DIM))
    tokens = lambda w: pl.BlockSpec((1, tm, w), lambda i, j: (i, j, 0))
    full = lambda a: pl.BlockSpec(a.shape, lambda i, j: (0,) * a.ndim)
    g_rec2 = g_rec.reshape(1, -1)
    g_ffn2 = g_ffn.reshape(1, d)
    return pl.pallas_call(
        functools.partial(_outproj_kernel, n_experts=n_experts),
        out_shape=[jax.ShapeDtypeStruct((b, l, d), F32),
                   jax.ShapeDtypeStruct((b, l, d), F32),
                   jax.ShapeDtypeStruct((b, n_experts, l), F32)],
        grid=(b, l // tm),
        in_specs=[tokens(xs.shape[2]), tokens(o_f.shape[2]), tokens(o_b.shape[2]), tokens(sg.shape[2]),
                  tokens(d), full(mod), full(g_rec2), full(cc), full(sc), full(w_fourier_bf16),
                  full(w_out_bf16), full(g_ffn2), full(w_router_pad)],
        out_specs=[tokens(d),
                   tokens(d),
                   pl.BlockSpec((1, n_experts, tm), lambda i, j: (i, 0, j))],
        compiler_params=pltpu.CompilerParams(
            dimension_semantics=("parallel", "parallel"), vmem_limit_bytes=VMEM_LIMIT),
        name="outproj",
    )(xs, o_f, o_b, sg, x, mod, g_rec2, cc, sc, w_fourier_bf16, w_out_bf16, g_ffn2, w_router_pad)


def _route_kernel(aff_ref, tri_ref, idx_ref, cum_ref, *, cap):
    a = aff_ref[0]
    ne, l = a.shape

    def value_bit(i, thr):
        cand = thr | jnp.left_shift(jnp.int32(1), 30 - i)
        cnt = jnp.sum(jnp.where(a >= pltpu.bitcast(cand, F32), 1.0, 0.0), axis=1, keepdims=True)
        return jnp.where(cnt >= cap, cand, thr)

    thr = lax.fori_loop(0, 31, value_bit, jnp.zeros((ne, 1), I32))
    gt = a >= pltpu.bitcast(thr + 1, F32)
    eq = (a >= pltpu.bitcast(thr, F32)) & jnp.logical_not(gt)
    need = cap - jnp.sum(jnp.where(gt, 1.0, 0.0), axis=1, keepdims=True)
    pos = lax.broadcasted_iota(I32, (ne, l), 1)
    nbits = int(l).bit_length()

    def index_bit(i, x):
        cand = x + jnp.left_shift(jnp.int32(1), nbits - 1 - i)
        cnt = jnp.sum(jnp.where(eq & (pos < cand), 1.0, 0.0), axis=1, keepdims=True)
        return jnp.where(cnt < need, cand, x)

    last = lax.fori_loop(0, nbits, index_bit, jnp.zeros((ne, 1), I32))
    sel = (gt | (eq & (pos <= last))).astype(BF16)

    n_tiles = l // LANE
    cum_ref[:, n_tiles:, :] = jnp.zeros((ne, LANE - n_tiles, LANE), F32)
    lane = lax.broadcasted_iota(I32, (ne, LANE), 1)
    off = jnp.zeros((ne, 1), F32)
    ends = jnp.zeros((ne, LANE), F32)
    for c in range(n_tiles):
        p = _dot(sel[:, c * LANE:(c + 1) * LANE], tri_ref[...]) + off
        cum_ref[:, c, :] = p
        off = p[:, LANE - 1:LANE]
        ends = jnp.where(lane == c, off, ends)
    never = float(2 * l)
    ends = jnp.where(lane < n_tiles, ends, never)
    starts = jnp.where(lane == 0, 0.0, pltpu.roll(ends, 1, 1))

    ones = jnp.ones((8, LANE), BF16)
    slot = lax.broadcasted_iota(I32, (cap, LANE), 0).astype(F32)
    for e in range(ne):
        en = ends[e:e + 1, :]
        st = starts[e:e + 1, :]
        before = en <= slot
        holds = ((st <= slot) & (slot < en)).astype(BF16)
        hi, lo = _split(cum_ref[e])
        tile_cum = _dot(holds, hi) + _dot(holds, lo)
        counts = jnp.where(before, float(LANE), 0.0) + jnp.where(tile_cum <= slot, 1.0, 0.0)
        tok = _dot_nt(ones, counts.astype(BF16))
        idx_ref[0, e:e + 1, :] = tok[0:1].astype(I32)


def _route(aff, cap):
    b, ne, l = aff.shape
    assert l % LANE == 0 and l // LANE <= LANE
    tri = jnp.asarray(np.triu(np.ones((LANE, LANE))), BF16)
    return pl.pallas_call(
        functools.partial(_route_kernel, cap=cap),
        out_shape=jax.ShapeDtypeStruct((b, ne, cap), I32),
        grid=(b,),
        in_specs=[pl.BlockSpec((1, ne, l), lambda i: (i, 0, 0)),
                  pl.BlockSpec(tri.shape, lambda i: (0, 0))],
        out_specs=pl.BlockSpec((1, ne, cap), lambda i: (i, 0, 0)),
        scratch_shapes=[pltpu.VMEM((ne, LANE, LANE), F32)],
        compiler_params=pltpu.CompilerParams(
            dimension_semantics=("parallel",), vmem_limit_bytes=VMEM_LIMIT),
        name="route",
    )(aff, tri)


def _ffn_kernel(src_ref, h_hbm, wg_ref, wu_ref, wd_ref, y_ref, xg_ref, xs_ref, sem, *, ne, nb, nf):
    e = pl.program_id(0)
    f = pl.program_id(1)
    n_used = xs_ref.shape[0]
    rows_per_step = xg_ref.shape[2]
    n_rows = rows_per_step * nf
    slot = e % 2

    def start_row(expert, slot_, step, i):
        t = src_ref[expert * n_rows + step * rows_per_step + i]
        pltpu.make_async_copy(h_hbm.at[pl.ds(t, 1)], xg_ref.at[slot_, step, pl.ds(i, 1)], sem.at[slot_]).start()

    def wait_rows(slot_):
        for step in range(nf):
            pltpu.make_async_copy(h_hbm.at[pl.ds(0, rows_per_step)], xg_ref.at[slot_, step], sem.at[slot_]).wait()

    @pl.when((e == 0) & (f == 0))
    def _():
        def issue(r, carry):
            start_row(0, 0, r // rows_per_step, r % rows_per_step)
            return carry

        lax.fori_loop(0, n_rows, issue, 0)

    @pl.when(f == 0)
    def _():
        wait_rows(slot)
        for step in range(nf):
            lo = step * rows_per_step
            cnt = min(rows_per_step, n_used - lo)
            if cnt > 0:
                xs_ref[lo:lo + cnt, :] = xg_ref[slot, step, 0:cnt, :].astype(BF16)
        y_ref[...] = jnp.zeros(y_ref.shape, F32)

    nxt = jnp.minimum(e + 1, ne - 1)
    for i in range(rows_per_step):
        start_row(nxt, 1 - slot, f, i)

    cap = n_used // nb

    wg = wg_ref[0].astype(BF16)
    wu = wu_ref[0].astype(BF16)
    wd = wd_ref[0].astype(BF16)
    for gb in range(nb):
        xs = xs_ref[gb * cap:(gb + 1) * cap, :]
        hid = _silu(_dot(xs, wg)) * _dot(xs, wu)
        y_ref[gb, 0] += _dot(hid.astype(BF16), wd)

    @pl.when((e == ne - 1) & (f == nf - 1))
    def _():
        wait_rows(1 - slot)


def _ffn(idx, h_rows, w_gate, w_up, w_down, *, tf):
    batch, _, cap = idx.shape
    ne, d, dff = w_gate.shape
    n_tokens = h_rows.shape[0] // batch
    nf = dff // tf
    rows_per_step = -(-(batch * cap) // (8 * nf)) * 8
    n_rows = rows_per_step * nf
    src = idx + (jnp.arange(batch, dtype=I32) * n_tokens)[:, None, None]
    src = jnp.transpose(src, (1, 0, 2)).reshape(ne, batch * cap)
    src = jnp.pad(src, ((0, 0), (0, n_rows - batch * cap)), mode="edge").reshape(-1)
    return pl.pallas_call(
        functools.partial(_ffn_kernel, ne=ne, nb=batch, nf=nf),
        out_shape=jax.ShapeDtypeStruct((batch, ne, cap, d), F32),
        grid_spec=pltpu.PrefetchScalarGridSpec(
            num_scalar_prefetch=1,
            grid=(ne, nf),
            in_specs=[pl.BlockSpec(memory_space=pl.ANY),
                      pl.BlockSpec((1, d, tf), lambda e, f, idx: (e, 0, f)),
                      pl.BlockSpec((1, d, tf), lambda e, f, idx: (e, 0, f)),
                      pl.BlockSpec((1, tf, d), lambda e, f, idx: (e, f, 0))],
            out_specs=pl.BlockSpec((batch, 1, cap, d), lambda e, f, idx: (0, e, 0, 0)),
            scratch_shapes=[pltpu.VMEM((2, nf, rows_per_step, d), F32),
                            pltpu.VMEM((batch * cap, d), BF16),
                            pltpu.SemaphoreType.DMA((2,))],
        ),
        compiler_params=pltpu.CompilerParams(
            dimension_semantics=("arbitrary", "arbitrary"), vmem_limit_bytes=VMEM_LIMIT),
        name="ffn",
    )(src, h_rows, w_gate, w_up, w_down)


def _combine_kernel(idx_ref, aff_ref, y_ref, o_hbm, acc_ref, sem):
    b = pl.program_id(0)
    e = pl.program_id(1)
    ne = pl.num_programs(1)
    cap = y_ref.shape[2]

    @pl.when(e == 0)
    def _():
        acc_ref[...] = jnp.zeros(acc_ref.shape, F32)

    def add_row(j, carry):
        t = idx_ref[(b * ne + e) * cap + j]
        acc_ref[pl.ds(t, 1), :] += y_ref[0, 0, pl.ds(j, 1), :] * aff_ref[0, 0, t]
        return carry

    lax.fori_loop(0, cap, add_row, 0, unroll=8)

    @pl.when(e == ne - 1)
    def _():
        out = pltpu.make_async_copy(acc_ref, o_hbm.at[b], sem)
        out.start()
        out.wait()


def _combine(idx_flat, aff_rows, ye, n_tokens):
    b, ne, cap, d = ye.shape
    return pl.pallas_call(
        _combine_kernel,
        out_shape=jax.ShapeDtypeStruct((b, n_tokens, d), F32),
        grid_spec=pltpu.PrefetchScalarGridSpec(
            num_scalar_prefetch=1,
            grid=(b, ne),
            in_specs=[pl.BlockSpec((1, 1, n_tokens), lambda i, e, idx: (i * ne + e, 0, 0),
                                   memory_space=pltpu.SMEM),
                      pl.BlockSpec((1, 1, cap, d), lambda i, e, idx: (i, e, 0, 0))],
            out_specs=pl.BlockSpec(memory_space=pl.ANY),
            scratch_shapes=[pltpu.VMEM((n_tokens, d), F32), pltpu.SemaphoreType.DMA],
        ),
        compiler_params=pltpu.CompilerParams(
            dimension_semantics=("arbitrary", "arbitrary"), vmem_limit_bytes=VMEM_LIMIT),
        name="combine",
    )(idx_flat, aff_rows, ye)


def _final_kernel(x1_ref, moe_ref, mod_ref, g_ref, o_ref):
    d = x1_ref.shape[2]
    gate2 = mod_ref[pl.ds(pl.program_id(0), 1), 5 * d:6 * d]
    o_ref[0] = _rms(x1_ref[0] + gate2 * moe_ref[0]) * g_ref[...]


def _final(x1, moe, mod, g_final, tm):
    b, l, d = x1.shape
    tokens = pl.BlockSpec((1, tm, d), lambda i, j: (i, j, 0))
    return pl.pallas_call(
        _final_kernel,
        out_shape=jax.ShapeDtypeStruct((b, l, d), F32),
        grid=(b, l // tm),
        in_specs=[tokens, tokens, pl.BlockSpec(mod.shape, lambda i, j: (0, 0)),
                  pl.BlockSpec((1, d), lambda i, j: (0, 0))],
        out_specs=tokens,
        compiler_params=pltpu.CompilerParams(
            dimension_semantics=("parallel", "parallel"), vmem_limit_bytes=VMEM_LIMIT),
        name="final",
    )(x1, moe, mod, g_final.reshape(1, d))


def kernel(x, c, ctx, c_ctx, w_ada, b_ada, g_mix, w_in, w_fourier, lb_logits, g_rec, w_out, g_ffn, w_router, w_exp_gate, w_exp_up, w_exp_down, g_final):
    b, l, d = x.shape
    depth = w_ada.shape[0]
    assert depth == 1 and lb_logits.shape[0] == 2 and l == DFT_ROWS * DFT_COLS
    d_f = FOURIER_GROUPS * GROUP_DIM
    d_r = REC_HEADS * HEAD_DIM
    ne = w_router.shape[2]
    cap = EC_CAPACITY_FACTOR * l // ne

    cond = jnp.concatenate([c, c_ctx[None], jnp.zeros((8 - b - 1, d), F32)], axis=0)
    mod = _ada(cond, w_ada[0], b_ada[0])

    w_in_b = w_in[0].astype(BF16)
    col = lambda i: d_f + i * d_r
    lat_parts = (("raw", 0, F32), ("silu", col(0), BF16), ("raw", col(1), BF16), ("logf_fwd", col(2), F32),
                 ("logf_bwd", col(3), F32), ("silu", col(4), BF16))
    u, q, v, lcf, lcb, sg = _inproj(x, mod, g_mix[0], w_in_b, lb_logits, ctx_row=None, parts=lat_parts, tm=512)
    q_c, v_c, lcf_c, lcb_c = _inproj(ctx, mod, g_mix[0], w_in_b, lb_logits, ctx_row=b, parts=lat_parts[1:5],
                                     tm=ctx.shape[1])

    f1, twc, tws, f3a, f3b = _dft_tables()
    t4 = _dft1(u.reshape(b, DFT_ROWS, DFT_COLS, d_f), f1, twc, tws)
    xs = _dft3(t4, f3a, f3b).reshape(b, l, 2 * d_f)

    zero = jnp.zeros((b, REC_HEADS, HEAD_DIM, HEAD_DIM), F32)
    s_f, s_b = _rec(q_c, v_c, lcf_c, lcb_c, zero, zero, emit_out=False)
    o_f, o_b = _rec(q, v, lcf, lcb, s_f, s_b, emit_out=True)

    w_router_pad = jnp.pad(w_router[0], ((0, 0), (0, LANE - ne)))
    x1, h2, aff = _outproj(xs, o_f, o_b, sg, x, mod, g_rec[0], w_fourier[0].astype(BF16),
                           w_out[0].astype(BF16), g_ffn[0], w_router_pad, n_experts=ne, tm=512)

    idx = _route(aff, cap)
    ye = _ffn(idx, h2.reshape(b * l, d), w_exp_gate[0], w_exp_up[0], w_exp_down[0], tf=256)
    moe = _combine(idx.reshape(-1), aff.reshape(b * ne, 1, l), ye, l)
    return _final(x1, moe, mod, g_final, 512)
```

```python
import functools

import numpy as np
import jax
import jax.numpy as jnp
from jax import lax
from jax.experimental import pallas as pl
from jax.experimental.pallas import tpu as pltpu

F32 = jnp.float32
BF16 = jnp.bfloat16
I32 = jnp.int32

NORM_EPS = 1e-6
LOG2_E = 1.4426950408889634
LANE = 128
N_ADA = 6
FOURIER_GROUPS = 4
GROUP_DIM = 128
REC_HEADS = 4
HEAD_DIM = 128
N_EXPERTS = 16
EC_CAPACITY_FACTOR = 2
DFT_ROWS = 128
DFT_COLS = 64
REC_CHUNK = 128
OUTPROJ_ROW_GROUPS = 2
VMEM_LIMIT = 56 * 1024 * 1024


def _dot(a, b):
    return jnp.dot(a, b, preferred_element_type=F32)


def _dot_nt(a, b):
    return lax.dot_general(a, b, (((1,), (1,)), ((), ())), preferred_element_type=F32)


def _dot_tn(a, b):
    return lax.dot_general(a, b, (((0,), (0,)), ((), ())), preferred_element_type=F32)


def _split(x):
    hi = x.astype(BF16)
    lo = (x - hi.astype(F32)).astype(BF16)
    return hi, lo


def _dot3(a, b):
    a_hi, a_lo = _split(a)
    b_hi, b_lo = _split(b)
    return _dot(a_hi, b_hi) + _dot(a_lo, b_hi) + _dot(a_hi, b_lo)


def _silu(x):
    return x * jax.nn.sigmoid(x)


def _rms(x):
    return x * lax.rsqrt(jnp.mean(x * x, axis=-1, keepdims=True) + NORM_EPS)


def _bf16_table(a):
    return jnp.asarray(a, F32).astype(BF16)


def _ada_kernel(cond_ref, w_ref, b_ref, o_ref):
    o_ref[...] = _dot3(_silu(cond_ref[...]), w_ref[...]) + b_ref[...]


def _ada(cond, w, b):
    d = cond.shape[1]
    n = w.shape[1]
    return pl.pallas_call(
        _ada_kernel,
        out_shape=jax.ShapeDtypeStruct((cond.shape[0], n), F32),
        grid=(n // d,),
        in_specs=[pl.BlockSpec(cond.shape, lambda j: (0, 0)),
                  pl.BlockSpec((d, d), lambda j: (0, j)),
                  pl.BlockSpec((1, d), lambda j: (0, j))],
        out_specs=pl.BlockSpec((cond.shape[0], d), lambda j: (0, j)),
        compiler_params=pltpu.CompilerParams(vmem_limit_bytes=VMEM_LIMIT),
        name="ada",
    )(cond, w, b.reshape(1, n))


def _inproj_kernel(x_ref, mod_ref, gmix_ref, w_ref, lbl_ref, *out_refs, ctx_row, parts):
    d = x_ref.shape[2]
    row = ctx_row if ctx_row is not None else pl.program_id(0)
    shift = mod_ref[pl.ds(row, 1), 0:d]
    scale = mod_ref[pl.ds(row, 1), d:2 * d]
    h = _rms(x_ref[0]) * gmix_ref[...] * (1.0 + scale) + shift
    hb = h.astype(BF16)

    l0 = lbl_ref[0]
    l1 = lbl_ref[1]
    m = jnp.maximum(l0, l1)
    e0 = jnp.exp(l0 - m)
    e1 = jnp.exp(l1 - m)
    lb = e0 / (e0 + e1)

    for o_ref, (kind, col, _) in zip(out_refs, parts):
        w = o_ref.shape[2]
        p = _dot(hb, w_ref[:, col:col + w])
        if kind == "raw":
            o_ref[0] = p.astype(o_ref.dtype)
        elif kind == "silu":
            o_ref[0] = _silu(p).astype(o_ref.dtype)
        else:
            lbd = lb[0:1] if kind == "logf_fwd" else lb[1:2]
            o_ref[0] = jnp.log(lbd + (1.0 - lbd) * jax.nn.sigmoid(p))


def _inproj(x, mod, g_mix, w_bf16, lb_logits, *, ctx_row, parts, tm):
    b, l, d = x.shape
    width = 512
    kern = functools.partial(_inproj_kernel, ctx_row=ctx_row, parts=parts)
    return pl.pallas_call(
        kern,
        out_shape=[jax.ShapeDtypeStruct((b, l, width), dt) for _, _, dt in parts],
        grid=(b, l // tm),
        in_specs=[pl.BlockSpec((1, tm, d), lambda i, j: (i, j, 0)),
                  pl.BlockSpec(mod.shape, lambda i, j: (0, 0)),
                  pl.BlockSpec((1, d), lambda i, j: (0, 0)),
                  pl.BlockSpec(w_bf16.shape, lambda i, j: (0, 0)),
                  pl.BlockSpec(lb_logits.shape, lambda i, j: (0, 0, 0))],
        out_specs=[pl.BlockSpec((1, tm, width), lambda i, j: (i, j, 0)) for _ in parts],
        compiler_params=pltpu.CompilerParams(
            dimension_semantics=("parallel", "parallel"), vmem_limit_bytes=VMEM_LIMIT),
        name="inproj_ctx" if ctx_row is not None else "inproj",
    )(x, mod, g_mix.reshape(1, d), w_bf16, lb_logits)


def _dft_tables():
    n1, n2 = DFT_ROWS, DFT_COLS
    k1 = np.arange(n1)
    a1 = 2.0 * np.pi * np.outer(k1, k1) / n1
    f1 = np.concatenate([np.cos(a1), -np.sin(a1)], axis=0)
    c = np.arange(n2)
    th = 2.0 * np.pi * np.outer(c, k1) / (n1 * n2)
    twc = np.repeat(np.cos(th)[:, :, None], LANE, axis=2)
    tws = np.repeat(np.sin(th)[:, :, None], LANE, axis=2)
    a2 = 2.0 * np.pi * np.outer(c, c) / n2
    f3a = np.concatenate([np.cos(a2), -np.sin(a2)], axis=0)
    f3b = np.concatenate([np.sin(a2), np.cos(a2)], axis=0)
    return (_bf16_table(f1), jnp.asarray(twc, F32), jnp.asarray(tws, F32),
            _bf16_table(f3a), _bf16_table(f3b))


def _dft1_kernel(*refs):
    n_slab = (len(refs) - 3) // 3
    x_refs = refs[:n_slab]
    f1_ref, twc_ref, tws_ref = refs[n_slab:n_slab + 3]
    re_refs = refs[n_slab + 3:2 * n_slab + 3]
    im_refs = refs[2 * n_slab + 3:]
    n1, cb = x_refs[0].shape[1:3]
    flat = lambda ref: ref.at[0].reshape(n1 * cb, LANE)
    xs, res, ims = [[flat(r) for r in group] for group in (x_refs, re_refs, im_refs)]
    for cc in range(cb):
        rows = pl.ds(cc, n1, stride=cb)
        xc = jnp.concatenate([x[rows, :] for x in xs], axis=1).astype(BF16)
        g = _dot(f1_ref[...], xc)
        tre = g[:n1]
        tim = g[n1:]
        cw = twc_ref[cc]
        sw = tws_ref[cc]
        for s in range(n_slab):
            cols = slice(s * LANE, (s + 1) * LANE)
            res[s][rows, :] = tre[:, cols] * cw + tim[:, cols] * sw
            ims[s][rows, :] = tim[:, cols] * cw - tre[:, cols] * sw


def _dft1(u4, f1, twc, tws):
    b, n1, n2, ch = u4.shape
    cb = 8
    n_slab = ch // LANE
    slab = lambda s: pl.BlockSpec((1, n1, cb, LANE), lambda i, j, s=s: (i, 0, j, s))
    return pl.pallas_call(
        _dft1_kernel,
        out_shape=[jax.ShapeDtypeStruct((b, n1, n2, LANE), F32)] * (2 * n_slab),
        grid=(b, n2 // cb),
        in_specs=[slab(s) for s in range(n_slab)] + [
            pl.BlockSpec(f1.shape, lambda i, j: (0, 0)),
            pl.BlockSpec((cb, n1, LANE), lambda i, j: (j, 0, 0)),
            pl.BlockSpec((cb, n1, LANE), lambda i, j: (j, 0, 0))],
        out_specs=[pl.BlockSpec((1, n1, cb, LANE), lambda i, j: (i, 0, j, 0))] * (2 * n_slab),
        compiler_params=pltpu.CompilerParams(
            dimension_semantics=("parallel", "parallel"), vmem_limit_bytes=VMEM_LIMIT),
        name="dft1",
    )(*([u4] * n_slab), f1, twc, tws)


def _dft3_kernel(*refs):
    n_slab = (len(refs) - 3) // 2
    re_refs = refs[:n_slab]
    im_refs = refs[n_slab:2 * n_slab]
    f3a_ref, f3b_ref, o_ref = refs[2 * n_slab:]
    n2 = re_refs[0].shape[2]
    ch = n_slab * LANE
    for kk in range(re_refs[0].shape[1]):
        tre = jnp.concatenate([r[0, kk] for r in re_refs], axis=1).astype(BF16)
        tim = jnp.concatenate([r[0, kk] for r in im_refs], axis=1).astype(BF16)
        x = _dot(f3a_ref[...], tre) + _dot(f3b_ref[...], tim)
        o_ref[0, :, kk, 0:ch] = x[:n2]
        o_ref[0, :, kk, ch:2 * ch] = x[n2:]


def _dft3(t_slabs, f3a, f3b):
    n_slab = len(t_slabs) // 2
    b, n1, n2, _ = t_slabs[0].shape
    w = 2 * n_slab * LANE
    kb = 8
    return pl.pallas_call(
        _dft3_kernel,
        out_shape=jax.ShapeDtypeStruct((b, n2, n1, w), F32),
        grid=(b, n1 // kb),
        in_specs=[pl.BlockSpec((1, kb, n2, LANE), lambda i, j: (i, j, 0, 0))] * (2 * n_slab) + [
            pl.BlockSpec(f3a.shape, lambda i, j: (0, 0)),
            pl.BlockSpec(f3b.shape, lambda i, j: (0, 0))],
        out_specs=pl.BlockSpec((1, n2, kb, w), lambda i, j: (i, 0, j, 0)),
        compiler_params=pltpu.CompilerParams(
            dimension_semantics=("parallel", "parallel"), vmem_limit_bytes=VMEM_LIMIT),
        name="dft3",
    )(*t_slabs, f3a, f3b)


def _rec_tables(reverse):
    c = REC_CHUNK
    t = np.arange(c)[:, None]
    s = np.arange(c)[None, :]
    masks = [np.eye(c)]
    m = 1
    while m < c:
        masks.append(((t // (2 * m) == s // (2 * m)) & ((t // m) % 2 == 1) & ((s // m) % 2 == 0)).astype(np.float64))
        m *= 2
    cum = (s <= t).astype(np.float64)
    if reverse:
        cum = cum[::-1, ::-1]
        masks = [a[::-1, ::-1] for a in masks]
    return jnp.asarray(cum, BF16), jnp.asarray(np.stack(masks), F32)


def _level_logdecay(b, lc, m, reverse):
    c, d = lc.shape
    if m <= 2:
        row = lax.broadcasted_iota(I32, (c, d), 0)
        if m == 1:
            return jnp.where((row & 1) == (0 if reverse else 1), lc, 0.0)
        r = row & 3
        nxt = pltpu.roll(lc, c - 1, 0)
        prv = pltpu.roll(lc, 1, 0)
        if reverse:
            return jnp.where(r == 0, lc + nxt, jnp.where(r == 1, lc, jnp.where(r == 2, 0.0, prv)))
        return jnp.where(r == 0, nxt, jnp.where(r == 1, 0.0, jnp.where(r == 2, lc, lc + prv)))
    anchor_row = m if reverse else m - 1
    b3 = b.reshape(c // (2 * m), 2 * m, d)
    anchor = jnp.broadcast_to(b3[:, anchor_row:anchor_row + 1, :], b3.shape).reshape(c, d)
    diff = b - anchor
    return jnp.minimum(diff, -diff)


def _rec_direction(q, v, lc, st_ref, h, cum_ref, mask_ref, emit_out, reverse):
    c = REC_CHUNK
    lc = lc * LOG2_E
    k = 1.0 - jnp.exp2(lc)
    hi, lo = _split(lc)
    res = _dot(cum_ref[...], jnp.concatenate([hi, lo], axis=1))
    b = res[:, :HEAD_DIM] + res[:, HEAD_DIM:]
    last = 0 if reverse else c - 1
    b_last = b[last:last + 1, :]
    st = st_ref[h]
    kb = k.astype(BF16)
    o = None
    if emit_out:
        a = mask_ref[0] * _dot_nt(q, kb)
        m, i = 1, 1
        while m < c:
            dm = jnp.exp2(_level_logdecay(b, lc, m, reverse)).astype(BF16)
            a = a + mask_ref[i] * _dot_nt(q * dm, kb * dm)
            m, i = 2 * m, i + 1
        qd = q * jnp.exp2(b).astype(BF16)
        o = _dot_nt(qd, st.astype(BF16)) + _dot(a.astype(BF16), v)
    kd = kb * jnp.exp2(b_last - b).astype(BF16)
    st_ref[h] = st * jnp.exp2(b_last) + _dot_tn(v, kd)
    return o


def _rec_kernel(qf_ref, vf_ref, lf_ref, qb_ref, vb_ref, lb_ref, s0f_ref, s0b_ref,
                wf_ref, mf_ref, wb_ref, mb_ref, of_ref, ob_ref, sf_ref, sb_ref, *, emit_out):
    n = pl.program_id(1)

    @pl.when(n == 0)
    def _():
        sf_ref[...] = s0f_ref[0]
        sb_ref[...] = s0b_ref[0]

    for h in range(REC_HEADS):
        cols = slice(h * HEAD_DIM, (h + 1) * HEAD_DIM)
        o_f = _rec_direction(qf_ref[0, :, cols], vf_ref[0, :, cols], lf_ref[0, :, cols],
                             sf_ref, h, wf_ref, mf_ref, emit_out, False)
        o_b = _rec_direction(qb_ref[0, :, cols], vb_ref[0, :, cols], lb_ref[0, :, cols],
                             sb_ref, h, wb_ref, mb_ref, emit_out, True)
        if emit_out:
            of_ref[0, :, cols] = o_f.astype(of_ref.dtype)
            ob_ref[0, :, cols] = o_b.astype(ob_ref.dtype)
    if not emit_out:
        @pl.when(n == pl.num_programs(1) - 1)
        def _():
            of_ref[0] = sf_ref[...]
            ob_ref[0] = sb_ref[...]


def _rec(q, v, lcf, lcb, s0f, s0b, *, emit_out):
    b, l, w = q.shape
    c = REC_CHUNK
    n = l // c
    wf, mf = _rec_tables(False)
    wb, mb = _rec_tables(True)
    fwd = lambda i, j: (i, j, 0)
    bwd = lambda i, j: (i, n - 1 - j, 0)
    tok = lambda im: pl.BlockSpec((1, c, w), im)
    state = pl.BlockSpec((1, REC_HEADS, HEAD_DIM, HEAD_DIM), lambda i, j: (i, 0, 0, 0))
    const2 = lambda a: pl.BlockSpec(a.shape, lambda i, j: (0, 0))
    const3 = lambda a: pl.BlockSpec(a.shape, lambda i, j: (0, 0, 0))
    if emit_out:
        out_shape = [jax.ShapeDtypeStruct((b, l, w), BF16)] * 2
        out_specs = [tok(fwd), tok(bwd)]
    else:
        out_shape = [jax.ShapeDtypeStruct((b, REC_HEADS, HEAD_DIM, HEAD_DIM), F32)] * 2
        out_specs = [state, state]
    state_scratch = pltpu.VMEM((REC_HEADS, HEAD_DIM, HEAD_DIM), F32)
    return pl.pallas_call(
        functools.partial(_rec_kernel, emit_out=emit_out),
        out_shape=out_shape,
        grid=(b, n),
        in_specs=[tok(fwd), tok(fwd), tok(fwd), tok(bwd), tok(bwd), tok(bwd), state, state,
                  const2(wf), const3(mf), const2(wb), const3(mb)],
        out_specs=out_specs,
        scratch_shapes=[state_scratch, state_scratch],
        compiler_params=pltpu.CompilerParams(
            dimension_semantics=("parallel", "arbitrary"), vmem_limit_bytes=VMEM_LIMIT),
        name="rec" if emit_out else "rec_ctx",
    )(q, v, lcf, q, v, lcb, s0f, s0b, wf, mf, wb, mb)


def _channel_dft_tables(scale):
    k = np.arange(GROUP_DIM)
    a = 2.0 * np.pi * np.outer(k, k) / GROUP_DIM
    return _bf16_table(np.cos(a) * scale), _bf16_table(np.sin(a) * scale)


def _outproj_kernel(xs_ref, of_ref, ob_ref, sg_ref, x_ref, mod_ref, grec_ref, cc_ref, sc_ref,
                    wf_ref, wo_ref, gffn_ref, wr_ref, x1_ref, h2_ref, aff_ref, *, n_experts):
    d = x_ref.shape[2]
    half = xs_ref.shape[2] // 2
    row = pl.program_id(0)
    gate1 = mod_ref[pl.ds(row, 1), 2 * d:3 * d]
    shift2 = mod_ref[pl.ds(row, 1), 3 * d:4 * d]
    scale2 = mod_ref[pl.ds(row, 1), 4 * d:5 * d]

    rows_per_group = x_ref.shape[1] // OUTPROJ_ROW_GROUPS
    for grp in range(OUTPROJ_ROW_GROUPS):
        rows = slice(grp * rows_per_group, (grp + 1) * rows_per_group)
        parts = []
        for g in range(FOURIER_GROUPS):
            lo = g * GROUP_DIM
            xre = xs_ref[0, rows, lo:lo + GROUP_DIM].astype(BF16)
            xim = xs_ref[0, rows, half + lo:half + lo + GROUP_DIM].astype(BF16)
            rez = _dot(xre, cc_ref[...]) + _dot(xim, sc_ref[...])
            parts.append(_dot(rez.astype(BF16), wf_ref[g]).astype(BF16))
        o = of_ref[0, rows, :].astype(F32) + ob_ref[0, rows, :].astype(F32)
        for h in range(REC_HEADS):
            lo = h * HEAD_DIM
            oh = _rms(o[:, lo:lo + HEAD_DIM]) * grec_ref[:, lo:lo + HEAD_DIM]
            parts.append((oh * sg_ref[0, rows, lo:lo + HEAD_DIM]).astype(BF16))
        mix = _dot(jnp.concatenate(parts, axis=1), wo_ref[...])
        x1 = x_ref[0, rows, :] + gate1 * mix
        x1_ref[0, rows, :] = x1

        h2 = _rms(x1) * gffn_ref[...] * (1.0 + scale2) + shift2
        h2_ref[0, rows, :] = h2
        logits = _dot3(h2, wr_ref[...])
        lane = lax.broadcasted_iota(I32, logits.shape, 1)
        logits = jnp.where(lane < n_experts, logits, -1e30)
        e = jnp.exp(logits - jnp.max(logits, axis=-1, keepdims=True))
        aff = e / jnp.sum(e, axis=-1, keepdims=True)
        aff_ref[0, :, rows] = aff.T[:n_experts]


def _outproj(xs, o_f, o_b, sg, x, mod, g_rec, w_fourier_bf16, w_out_bf16, g_ffn, w_router_pad, *, n_experts, tm):
    b, l, d = x.shape
    cc, sc = _channel_dft_tables(1.0 / np.sqrt(l * GROUP_DIM))
    tokens = lambda w: pl.BlockSpec((1, tm, w), lambda i, j: (i, j, 0))
    full = lambda a: pl.BlockSpec(a.shape, lambda i, j: (0,) * a.ndim)
    g_rec2 = g_rec.reshape(1, -1)
    g_ffn2 = g_ffn.reshape(1, d)
    return pl.pallas_call(
        functools.partial(_outproj_kernel, n_experts=n_experts),
        out_shape=[jax.ShapeDtypeStruct((b, l, d), F32),
                   jax.ShapeDtypeStruct((b, l, d), F32),
                   jax.ShapeDtypeStruct((b, n_experts, l), F32)],
        grid=(b, l // tm),
        in_specs=[tokens(xs.shape[2]), tokens(o_f.shape[2]), tokens(o_b.shape[2]), tokens(sg.shape[2]),
                  tokens(d), full(mod), full(g_rec2), full(cc), full(sc), full(w_fourier_bf16),
                  full(w_out_bf16), full(g_ffn2), full(w_router_pad)],
        out_specs=[tokens(d),
                   tokens(d),
                   pl.BlockSpec((1, n_experts, tm), lambda i, j: (i, 0, j))],
        compiler_params=pltpu.CompilerParams(
            dimension_semantics=("parallel", "parallel"), vmem_limit_bytes=VMEM_LIMIT),
        name="outproj",
    )(xs, o_f, o_b, sg, x, mod, g_rec2, cc, sc, w_fourier_bf16, w_out_bf16, g_ffn2, w_router_pad)


def _route_kernel(aff_ref, tri_ref, idx_ref, cum_ref, *, cap):
    a = aff_ref[0]
    ne, l = a.shape

    def value_bit(i, thr):
        cand = thr | jnp.left_shift(jnp.int32(1), 30 - i)
        cnt = jnp.sum(jnp.where(a >= pltpu.bitcast(cand, F32), 1.0, 0.0), axis=1, keepdims=True)
        return jnp.where(cnt >= cap, cand, thr)

    thr = lax.fori_loop(0, 31, value_bit, jnp.zeros((ne, 1), I32))
    gt = a >= pltpu.bitcast(thr + 1, F32)
    eq = (a >= pltpu.bitcast(thr, F32)) & jnp.logical_not(gt)
    need = cap - jnp.sum(jnp.where(gt, 1.0, 0.0), axis=1, keepdims=True)
    pos = lax.broadcasted_iota(I32, (ne, l), 1)
    nbits = int(l).bit_length()

    def index_bit(i, x):
        cand = x + jnp.left_shift(jnp.int32(1), nbits - 1 - i)
        cnt = jnp.sum(jnp.where(eq & (pos < cand), 1.0, 0.0), axis=1, keepdims=True)
        return jnp.where(cnt < need, cand, x)

    last = lax.fori_loop(0, nbits, index_bit, jnp.zeros((ne, 1), I32))
    sel = (gt | (eq & (pos <= last))).astype(BF16)

    n_tiles = l // LANE
    cum_ref[:, n_tiles:, :] = jnp.zeros((ne, LANE - n_tiles, LANE), F32)
    lane = lax.broadcasted_iota(I32, (ne, LANE), 1)
    off = jnp.zeros((ne, 1), F32)
    ends = jnp.zeros((ne, LANE), F32)
    for c in range(n_tiles):
        p = _dot(sel[:, c * LANE:(c + 1) * LANE], tri_ref[...]) + off
        cum_ref[:, c, :] = p
        off = p[:, LANE - 1:LANE]
        ends = jnp.where(lane == c, off, ends)
    never = float(2 * l)
    ends = jnp.where(lane < n_tiles, ends, never)
    starts = jnp.where(lane == 0, 0.0, pltpu.roll(ends, 1, 1))
    ones = jnp.ones((8, LANE), BF16)
    slot = lax.broadcasted_iota(I32, (cap, LANE), 0).astype(F32)
    for e in range(ne):
        en = ends[e:e + 1, :]
        st = starts[e:e + 1, :]
        before = en <= slot
        holds = ((st <= slot) & (slot < en)).astype(BF16)
        hi, lo = _split(cum_ref[e])
        tile_cum = _dot(holds, hi) + _dot(holds, lo)
        counts = jnp.where(before, float(LANE), 0.0) + jnp.where(tile_cum <= slot, 1.0, 0.0)
        tok = _dot_nt(ones, counts.astype(BF16))
        idx_ref[0, e:e + 1, :] = tok[0:1].astype(I32)


def _route(aff, cap):
    b, ne, l = aff.shape
    assert l % LANE == 0 and l // LANE <= LANE
    tri = jnp.asarray(np.triu(np.ones((LANE, LANE))), BF16)
    return pl.pallas_call(
        functools.partial(_route_kernel, cap=cap),
        out_shape=jax.ShapeDtypeStruct((b, ne, cap), I32),
        grid=(b,),
        in_specs=[pl.BlockSpec((1, ne, l), lambda i: (i, 0, 0)),
                  pl.BlockSpec(tri.shape, lambda i: (0, 0))],
        out_specs=pl.BlockSpec((1, ne, cap), lambda i: (i, 0, 0)),
        scratch_shapes=[pltpu.VMEM((ne, LANE, LANE), F32)],
        compiler_params=pltpu.CompilerParams(
            dimension_semantics=("parallel",), vmem_limit_bytes=VMEM_LIMIT),
        name="route",
    )(aff, tri)


def _ffn_kernel(src_ref, h_hbm, wg_ref, wu_ref, wd_ref, y_ref, xg_ref, xs_ref, sem, *, ne, nb, nf):
    e = pl.program_id(0)
    f = pl.program_id(1)
    n_used = xs_ref.shape[0]
    rows_per_step = xg_ref.shape[2]
    n_rows = rows_per_step * nf
    slot = e % 2

    def start_row(expert, slot_, step, i):
        t = src_ref[expert * n_rows + step * rows_per_step + i]
        pltpu.make_async_copy(h_hbm.at[pl.ds(t, 1)], xg_ref.at[slot_, step, pl.ds(i, 1)], sem.at[slot_]).start()

    def wait_rows(slot_):
        for step in range(nf):
            pltpu.make_async_copy(h_hbm.at[pl.ds(0, rows_per_step)], xg_ref.at[slot_, step], sem.at[slot_]).wait()

    @pl.when((e == 0) & (f == 0))
    def _():
        for step in range(nf):
            def issue(i, carry, step=step):
                start_row(0, 0, step, i)
                return carry

            lax.fori_loop(0, rows_per_step, issue, 0)

    @pl.when(f == 0)
    def _():
        wait_rows(slot)
        for step in range(nf):
            lo = step * rows_per_step
            cnt = min(rows_per_step, n_used - lo)
            if cnt > 0:
                xs_ref[lo:lo + cnt, :] = xg_ref[slot, step, 0:cnt, :].astype(BF16)
        y_ref[...] = jnp.zeros(y_ref.shape, F32)

    nxt = jnp.minimum(e + 1, ne - 1)
    for i in range(rows_per_step):
        start_row(nxt, 1 - slot, f, i)

    cap = n_used // nb

    wg = wg_ref[0].astype(BF16)
    wu = wu_ref[0].astype(BF16)
    wd = wd_ref[0].astype(BF16)
    for gb in range(nb):
        xs = xs_ref[gb * cap:(gb + 1) * cap, :]
        hid = _silu(_dot(xs, wg)) * _dot(xs, wu)
        y_ref[gb, 0] += _dot(hid.astype(BF16), wd)

    @pl.when((e == ne - 1) & (f == nf - 1))
    def _():
        wait_rows(1 - slot)


def _ffn(idx, h_rows, w_gate, w_up, w_down, *, tf):
    batch, _, cap = idx.shape
    ne, d, dff = w_gate.shape
    n_tokens = h_rows.shape[0] // batch
    nf = dff // tf
    rows_per_step = -(-(batch * cap) // (8 * nf)) * 8
    n_rows = rows_per_step * nf
    src = idx + (jnp.arange(batch, dtype=I32) * n_tokens)[:, None, None]
    src = jnp.transpose(src, (1, 0, 2)).reshape(ne, batch * cap)
    src = jnp.pad(src, ((0, 0), (0, n_rows - batch * cap)), mode="edge").reshape(-1)
    return pl.pallas_call(
        functools.partial(_ffn_kernel, ne=ne, nb=batch, nf=nf),
        out_shape=jax.ShapeDtypeStruct((batch, ne, cap, d), F32),
        grid_spec=pltpu.PrefetchScalarGridSpec(
            num_scalar_prefetch=1,
            grid=(ne, nf),
            in_specs=[pl.BlockSpec(memory_space=pl.ANY),
                      pl.BlockSpec((1, d, tf), lambda e, f, idx: (e, 0, f)),
                      pl.BlockSpec((1, d, tf), lambda e, f, idx: (e, 0, f)),
                      pl.BlockSpec((1, tf, d), lambda e, f, idx: (e, f, 0))],
            out_specs=pl.BlockSpec((batch, 1, cap, d), lambda e, f, idx: (0, e, 0, 0)),
            scratch_shapes=[pltpu.VMEM((2, nf, rows_per_step, d), F32),
                            pltpu.VMEM((batch * cap, d), BF16),
                            pltpu.SemaphoreType.DMA((2,))],
        ),
        compiler_params=pltpu.CompilerParams(
            dimension_semantics=("arbitrary", "arbitrary"), vmem_limit_bytes=VMEM_LIMIT),
        name="ffn",
    )(src, h_rows, w_gate, w_up, w_down)


COMBINE_GROUP = 8


def _combine_kernel(idx_ref, aff_ref, y_ref, o_hbm, acc_ref, sem):
    b = pl.program_id(0)
    e = pl.program_id(1)
    ne = pl.num_programs(1)
    cap = y_ref.shape[2]
    base = (b * ne + e) * cap

    @pl.when(e == 0)
    def _():
        acc_ref[...] = jnp.zeros(acc_ref.shape, F32)

    def add_rows(g, carry):
        j0 = pl.multiple_of(g * COMBINE_GROUP, COMBINE_GROUP)
        for u in range(COMBINE_GROUP):
            t = idx_ref[base + j0 + u]
            acc_ref[pl.ds(t, 1), :] += y_ref[0, 0, pl.ds(j0 + u, 1), :] * aff_ref[0, 0, t]
        return carry

    lax.fori_loop(0, cap // COMBINE_GROUP, add_rows, 0)

    @pl.when(e == ne - 1)
    def _():
        out = pltpu.make_async_copy(acc_ref, o_hbm.at[b], sem)
        out.start()
        out.wait()


def _combine(idx_flat, aff_rows, ye, n_tokens):
    b, ne, cap, d = ye.shape
    assert cap % COMBINE_GROUP == 0
    return pl.pallas_call(
        _combine_kernel,
        out_shape=jax.ShapeDtypeStruct((b, n_tokens, d), F32),
        grid_spec=pltpu.PrefetchScalarGridSpec(
            num_scalar_prefetch=1,
            grid=(b, ne),
            in_specs=[pl.BlockSpec((1, 1, n_tokens), lambda i, e, idx: (i * ne + e, 0, 0),
                                   memory_space=pltpu.SMEM),
                      pl.BlockSpec((1, 1, cap, d), lambda i, e, idx: (i, e, 0, 0))],
            out_specs=pl.BlockSpec(memory_space=pl.ANY),
            scratch_shapes=[pltpu.VMEM((n_tokens, d), F32), pltpu.SemaphoreType.DMA],
        ),
        compiler_params=pltpu.CompilerParams(
            dimension_semantics=("arbitrary", "arbitrary"), vmem_limit_bytes=VMEM_LIMIT),
        name="combine",
    )(idx_flat, aff_rows, ye)


def _final_kernel(x1_ref, moe_ref, mod_ref, g_ref, o_ref):
    d = x1_ref.shape[2]
    gate2 = mod_ref[pl.ds(pl.program_id(0), 1), 5 * d:6 * d]
    o_ref[0] = _rms(x1_ref[0] + gate2 * moe_ref[0]) * g_ref[...]


def _final(x1, moe, mod, g_final, tm):
    b, l, d = x1.shape
    tokens = pl.BlockSpec((1, tm, d), lambda i, j: (i, j, 0))
    return pl.pallas_call(
        _final_kernel,
        out_shape=jax.ShapeDtypeStruct((b, l, d), F32),
        grid=(b, l // tm),
        in_specs=[tokens, tokens, pl.BlockSpec(mod.shape, lambda i, j: (0, 0)),
                  pl.BlockSpec((1, d), lambda i, j: (0, 0))],
        out_specs=tokens,
        compiler_params=pltpu.CompilerParams(
            dimension_semantics=("parallel", "parallel"), vmem_limit_bytes=VMEM_LIMIT),
        name="final",
    )(x1, moe, mod, g_final.reshape(1, d))


def kernel(x, c, ctx, c_ctx, w_ada, b_ada, g_mix, w_in, w_fourier, lb_logits, g_rec, w_out, g_ffn, w_router, w_exp_gate, w_exp_up, w_exp_down, g_final):
    b, l, d = x.shape
    depth = w_ada.shape[0]
    assert depth == 1 and lb_logits.shape[0] == 2 and l == DFT_ROWS * DFT_COLS
    d_f = FOURIER_GROUPS * GROUP_DIM
    d_r = REC_HEADS * HEAD_DIM
    ne = w_router.shape[2]
    cap = EC_CAPACITY_FACTOR * l // ne

    cond = jnp.concatenate([c, c_ctx[None], jnp.zeros((8 - b - 1, d), F32)], axis=0)
    mod = _ada(cond, w_ada[0], b_ada[0])

    w_in_b = w_in[0].astype(BF16)
    col = lambda i: d_f + i * d_r
    lat_parts = (("raw", 0, F32), ("silu", col(0), BF16), ("raw", col(1), BF16), ("logf_fwd", col(2), F32),
                 ("logf_bwd", col(3), F32), ("silu", col(4), BF16))
    u, q, v, lcf, lcb, sg = _inproj(x, mod, g_mix[0], w_in_b, lb_logits, ctx_row=None, parts=lat_parts, tm=512)
    q_c, v_c, lcf_c, lcb_c = _inproj(ctx, mod, g_mix[0], w_in_b, lb_logits, ctx_row=b, parts=lat_parts[1:5],
                                     tm=ctx.shape[1])

    f1, twc, tws, f3a, f3b = _dft_tables()
    t4 = _dft1(u.reshape(b, DFT_ROWS, DFT_COLS, d_f), f1, twc, tws)
    xs = _dft3(t4, f3a, f3b).reshape(b, l, 2 * d_f)

    zero = jnp.zeros((b, REC_HEADS, HEAD_DIM, HEAD_DIM), F32)
    s_f, s_b = _rec(q_c, v_c, lcf_c, lcb_c, zero, zero, emit_out=False)
    o_f, o_b = _rec(q, v, lcf, lcb, s_f, s_b, emit_out=True)

    w_router_pad = jnp.pad(w_router[0], ((0, 0), (0, LANE - ne)))
    x1, h2, aff = _outproj(xs, o_f, o_b, sg, x, mod, g_rec[0], w_fourier[0].astype(BF16),
                           w_out[0].astype(BF16), g_ffn[0], w_router_pad, n_experts=ne, tm=512)

    idx = _route(aff, cap)
    ye = _ffn(idx, h2.reshape(b * l, d), w_exp_gate[0], w_exp_up[0], w_exp_down[0], tf=256)
    moe = _combine(idx.reshape(-1), aff.reshape(b * ne, 1, l), ye, l)
    return _final(x1, moe, mod, g_final, 512)
```

```python
import functools

import numpy as np
import jax
import jax.numpy as jnp
from jax import lax
from jax.experimental import pallas as pl
from jax.experimental.pallas import tpu as pltpu

F32 = jnp.float32
BF16 = jnp.bfloat16
I32 = jnp.int32

NORM_EPS = 1e-6
LOG2_E = 1.4426950408889634
MIN_NORMAL_BITS = 0x00800000
LANE = 128
N_ADA = 6
FOURIER_GROUPS = 4
GROUP_DIM = 128
REC_HEADS = 4
HEAD_DIM = 128
N_EXPERTS = 16
EC_CAPACITY_FACTOR = 2
DFT_ROWS = 128
DFT_COLS = 64
REC_CHUNK = 256
OUTPROJ_ROW_GROUPS = 2
VMEM_LIMIT = 56 * 1024 * 1024


def _dot(a, b):
    return jnp.dot(a, b, preferred_element_type=F32)


def _dot_nt(a, b):
    return lax.dot_general(a, b, (((1,), (1,)), ((), ())), preferred_element_type=F32)


def _dot_tn(a, b):
    return lax.dot_general(a, b, (((0,), (0,)), ((), ())), preferred_element_type=F32)


def _split(x):
    hi = x.astype(BF16)
    lo = (x - hi.astype(F32)).astype(BF16)
    return hi, lo


def _dot3(a, b):
    a_hi, a_lo = _split(a)
    b_hi, b_lo = _split(b)
    return _dot(a_hi, b_hi) + _dot(a_lo, b_hi) + _dot(a_hi, b_lo)


def _silu(x):
    return x * jax.nn.sigmoid(x)


def _rms(x):
    return x * lax.rsqrt(jnp.mean(x * x, axis=-1, keepdims=True) + NORM_EPS)


def _bf16_table(a):
    return jnp.asarray(a, F32).astype(BF16)


def _ada_kernel(cond_ref, w_ref, b_ref, o_ref):
    o_ref[...] = _dot3(_silu(cond_ref[...]), w_ref[...]) + b_ref[...]


def _ada(cond, w, b):
    d = cond.shape[1]
    n = w.shape[1]
    return pl.pallas_call(
        _ada_kernel,
        out_shape=jax.ShapeDtypeStruct((cond.shape[0], n), F32),
        grid=(n // d,),
        in_specs=[pl.BlockSpec(cond.shape, lambda j: (0, 0)),
                  pl.BlockSpec((d, d), lambda j: (0, j)),
                  pl.BlockSpec((1, d), lambda j: (0, j))],
        out_specs=pl.BlockSpec((cond.shape[0], d), lambda j: (0, j)),
        compiler_params=pltpu.CompilerParams(vmem_limit_bytes=VMEM_LIMIT),
        name="ada",
    )(cond, w, b.reshape(1, n))


def _inproj_kernel(x_ref, mod_ref, gmix_ref, w_ref, lbl_ref, *out_refs, ctx_row, parts):
    d = x_ref.shape[2]
    row = ctx_row if ctx_row is not None else pl.program_id(0)
    shift = mod_ref[pl.ds(row, 1), 0:d]
    scale = mod_ref[pl.ds(row, 1), d:2 * d]
    h = _rms(x_ref[0]) * gmix_ref[...] * (1.0 + scale) + shift
    hb = h.astype(BF16)

    l0 = lbl_ref[0]
    l1 = lbl_ref[1]
    m = jnp.maximum(l0, l1)
    e0 = jnp.exp(l0 - m)
    e1 = jnp.exp(l1 - m)
    lb = e0 / (e0 + e1)

    for o_ref, (kind, col, _) in zip(out_refs, parts):
        w = o_ref.shape[2]
        p = _dot(hb, w_ref[:, col:col + w])
        if kind == "raw":
            o_ref[0] = p.astype(o_ref.dtype)
        elif kind == "silu":
            o_ref[0] = _silu(p).astype(o_ref.dtype)
        else:
            lbd = lb[0:1] if kind == "logf_fwd" else lb[1:2]
            o_ref[0] = jnp.log(lbd + (1.0 - lbd) * jax.nn.sigmoid(p))


def _inproj(x, mod, g_mix, w_bf16, lb_logits, *, ctx_row, parts, tm):
    b, l, d = x.shape
    width = 512
    kern = functools.partial(_inproj_kernel, ctx_row=ctx_row, parts=parts)
    return pl.pallas_call(
        kern,
        out_shape=[jax.ShapeDtypeStruct((b, l, width), dt) for _, _, dt in parts],
        grid=(b, l // tm),
        in_specs=[pl.BlockSpec((1, tm, d), lambda i, j: (i, j, 0)),
                  pl.BlockSpec(mod.shape, lambda i, j: (0, 0)),
                  pl.BlockSpec((1, d), lambda i, j: (0, 0)),
                  pl.BlockSpec(w_bf16.shape, lambda i, j: (0, 0)),
                  pl.BlockSpec(lb_logits.shape, lambda i, j: (0, 0, 0))],
        out_specs=[pl.BlockSpec((1, tm, width), lambda i, j: (i, j, 0)) for _ in parts],
        compiler_params=pltpu.CompilerParams(
            dimension_semantics=("parallel", "parallel"), vmem_limit_bytes=VMEM_LIMIT),
        name="inproj_ctx" if ctx_row is not None else "inproj",
    )(x, mod, g_mix.reshape(1, d), w_bf16, lb_logits)


def _dft_tables():
    n1, n2 = DFT_ROWS, DFT_COLS
    k1 = np.arange(n1)
    a1 = 2.0 * np.pi * np.outer(k1, k1) / n1
    f1 = np.concatenate([np.cos(a1), -np.sin(a1)], axis=0)
    c = np.arange(n2)
    th = 2.0 * np.pi * np.outer(c, k1) / (n1 * n2)
    twc = np.repeat(np.cos(th)[:, :, None], LANE, axis=2)
    tws = np.repeat(np.sin(th)[:, :, None], LANE, axis=2)
    a2 = 2.0 * np.pi * np.outer(c, c) / n2
    f3a = np.concatenate([np.cos(a2), -np.sin(a2)], axis=0)
    f3b = np.concatenate([np.sin(a2), np.cos(a2)], axis=0)
    return (_bf16_table(f1), jnp.asarray(twc, F32), jnp.asarray(tws, F32),
            _bf16_table(f3a), _bf16_table(f3b))


def _dft1_kernel(*refs):
    n_slab = (len(refs) - 3) // 3
    x_refs = refs[:n_slab]
    f1_ref, twc_ref, tws_ref = refs[n_slab:n_slab + 3]
    re_refs = refs[n_slab + 3:2 * n_slab + 3]
    im_refs = refs[2 * n_slab + 3:]
    n1, cb = x_refs[0].shape[1:3]
    flat = lambda ref: ref.at[0].reshape(n1 * cb, LANE)
    xs, res, ims = [[flat(r) for r in group] for group in (x_refs, re_refs, im_refs)]
    for cc in range(cb):
        rows = pl.ds(cc, n1, stride=cb)
        xc = jnp.concatenate([x[rows, :] for x in xs], axis=1).astype(BF16)
        g = _dot(f1_ref[...], xc)
        tre = g[:n1]
        tim = g[n1:]
        cw = twc_ref[cc]
        sw = tws_ref[cc]
        for s in range(n_slab):
            cols = slice(s * LANE, (s + 1) * LANE)
            res[s][rows, :] = tre[:, cols] * cw + tim[:, cols] * sw
            ims[s][rows, :] = tim[:, cols] * cw - tre[:, cols] * sw


def _dft1(u4, f1, twc, tws):
    b, n1, n2, ch = u4.shape
    cb = 8
    n_slab = ch // LANE
    slab = lambda s: pl.BlockSpec((1, n1, cb, LANE), lambda i, j, s=s: (i, 0, j, s))
    return pl.pallas_call(
        _dft1_kernel,
        out_shape=[jax.ShapeDtypeStruct((b, n1, n2, LANE), F32)] * (2 * n_slab),
        grid=(b, n2 // cb),
        in_specs=[slab(s) for s in range(n_slab)] + [
            pl.BlockSpec(f1.shape, lambda i, j: (0, 0)),
            pl.BlockSpec((cb, n1, LANE), lambda i, j: (j, 0, 0)),
            pl.BlockSpec((cb, n1, LANE), lambda i, j: (j, 0, 0))],
        out_specs=[pl.BlockSpec((1, n1, cb, LANE), lambda i, j: (i, 0, j, 0))] * (2 * n_slab),
        compiler_params=pltpu.CompilerParams(
            dimension_semantics=("parallel", "parallel"), vmem_limit_bytes=VMEM_LIMIT),
        name="dft1",
    )(*([u4] * n_slab), f1, twc, tws)


def _dft3_kernel(*refs):
    n_slab = (len(refs) - 3) // 2
    re_refs = refs[:n_slab]
    im_refs = refs[n_slab:2 * n_slab]
    f3a_ref, f3b_ref, o_ref = refs[2 * n_slab:]
    n2 = re_refs[0].shape[2]
    ch = n_slab * LANE
    for kk in range(re_refs[0].shape[1]):
        tre = jnp.concatenate([r[0, kk] for r in re_refs], axis=1).astype(BF16)
        tim = jnp.concatenate([r[0, kk] for r in im_refs], axis=1).astype(BF16)
        x = _dot(f3a_ref[...], tre) + _dot(f3b_ref[...], tim)
        o_ref[0, :, kk, 0:ch] = x[:n2]
        o_ref[0, :, kk, ch:2 * ch] = x[n2:]


def _dft3(t_slabs, f3a, f3b):
    n_slab = len(t_slabs) // 2
    b, n1, n2, _ = t_slabs[0].shape
    w = 2 * n_slab * LANE
    kb = 8
    return pl.pallas_call(
        _dft3_kernel,
        out_shape=jax.ShapeDtypeStruct((b, n2, n1, w), F32),
        grid=(b, n1 // kb),
        in_specs=[pl.BlockSpec((1, kb, n2, LANE), lambda i, j: (i, j, 0, 0))] * (2 * n_slab) + [
            pl.BlockSpec(f3a.shape, lambda i, j: (0, 0)),
            pl.BlockSpec(f3b.shape, lambda i, j: (0, 0))],
        out_specs=pl.BlockSpec((1, n2, kb, w), lambda i, j: (i, 0, j, 0)),
        compiler_params=pltpu.CompilerParams(
            dimension_semantics=("parallel", "parallel"), vmem_limit_bytes=VMEM_LIMIT),
        name="dft3",
    )(*t_slabs, f3a, f3b)


def _rec_tables(reverse):
    c = REC_CHUNK
    t = np.arange(c)[:, None]
    s = np.arange(c)[None, :]
    masks = [np.eye(c)]
    m = 1
    while m < c:
        masks.append(((t // (2 * m) == s // (2 * m)) & ((t // m) % 2 == 1) & ((s // m) % 2 == 0)).astype(np.float64))
        m *= 2
    cum = (s <= t).astype(np.float64)
    if reverse:
        cum = cum[::-1, ::-1]
        masks = [a[::-1, ::-1] for a in masks]
    return jnp.asarray(cum, BF16), jnp.asarray(np.stack(masks), F32)


def _level_logdecay(b, lc, m, reverse):
    c, d = lc.shape
    if m <= 2:
        row = lax.broadcasted_iota(I32, (c, d), 0)
        if m == 1:
            return jnp.where((row & 1) == (0 if reverse else 1), lc, 0.0)
        r = row & 3
        nxt = pltpu.roll(lc, c - 1, 0)
        prv = pltpu.roll(lc, 1, 0)
        if reverse:
            return jnp.where(r == 0, lc + nxt, jnp.where(r == 1, lc, jnp.where(r == 2, 0.0, prv)))
        return jnp.where(r == 0, nxt, jnp.where(r == 1, 0.0, jnp.where(r == 2, lc, lc + prv)))
    anchor_row = m if reverse else m - 1
    b3 = b.reshape(c // (2 * m), 2 * m, d)
    anchor = jnp.broadcast_to(b3[:, anchor_row:anchor_row + 1, :], b3.shape).reshape(c, d)
    diff = b - anchor
    return jnp.minimum(diff, -diff)


def _rec_direction(q, v, lc, st_ref, h, cum_ref, mask_ref, emit_out, reverse):
    c = REC_CHUNK
    lc = lc * LOG2_E
    k = 1.0 - jnp.exp2(lc)
    hi, lo = _split(lc)
    res = _dot(cum_ref[...], jnp.concatenate([hi, lo], axis=1))
    b = res[:, :HEAD_DIM] + res[:, HEAD_DIM:]
    last = 0 if reverse else c - 1
    b_last = b[last:last + 1, :]
    st = st_ref[h]
    kb = k.astype(BF16)
    o = None
    if emit_out:
        a = mask_ref[0] * _dot_nt(q, kb)
        m, i = 1, 1
        while m < c:
            dm = jnp.exp2(_level_logdecay(b, lc, m, reverse)).astype(BF16)
            a = a + mask_ref[i] * _dot_nt(q * dm, kb * dm)
            m, i = 2 * m, i + 1
        qd = q * jnp.exp2(b).astype(BF16)
        o = _dot_nt(qd, st.astype(BF16)) + _dot(a.astype(BF16), v)
    kd = kb * jnp.exp2(b_last - b).astype(BF16)
    st_ref[h] = st * jnp.exp2(b_last) + _dot_tn(v, kd)
    return o


def _rec_kernel(qf_ref, vf_ref, lf_ref, qb_ref, vb_ref, lb_ref, s0f_ref, s0b_ref,
                wf_ref, mf_ref, wb_ref, mb_ref, of_ref, ob_ref, sf_ref, sb_ref, *, emit_out):
    n = pl.program_id(1)

    @pl.when(n == 0)
    def _():
        sf_ref[...] = s0f_ref[0]
        sb_ref[...] = s0b_ref[0]

    for h in range(REC_HEADS):
        cols = slice(h * HEAD_DIM, (h + 1) * HEAD_DIM)
        o_f = _rec_direction(qf_ref[0, :, cols], vf_ref[0, :, cols], lf_ref[0, :, cols],
                             sf_ref, h, wf_ref, mf_ref, emit_out, False)
        o_b = _rec_direction(qb_ref[0, :, cols], vb_ref[0, :, cols], lb_ref[0, :, cols],
                             sb_ref, h, wb_ref, mb_ref, emit_out, True)
        if emit_out:
            of_ref[0, :, cols] = o_f.astype(of_ref.dtype)
            ob_ref[0, :, cols] = o_b.astype(ob_ref.dtype)
    if not emit_out:
        @pl.when(n == pl.num_programs(1) - 1)
        def _():
            of_ref[0] = sf_ref[...]
            ob_ref[0] = sb_ref[...]


def _rec(q, v, lcf, lcb, s0f, s0b, *, emit_out):
    b, l, w = q.shape
    c = REC_CHUNK
    n = l // c
    wf, mf = _rec_tables(False)
    wb, mb = _rec_tables(True)
    fwd = lambda i, j: (i, j, 0)
    bwd = lambda i, j: (i, n - 1 - j, 0)
    tok = lambda im: pl.BlockSpec((1, c, w), im)
    state = pl.BlockSpec((1, REC_HEADS, HEAD_DIM, HEAD_DIM), lambda i, j: (i, 0, 0, 0))
    const2 = lambda a: pl.BlockSpec(a.shape, lambda i, j: (0, 0))
    const3 = lambda a: pl.BlockSpec(a.shape, lambda i, j: (0, 0, 0))
    if emit_out:
        out_shape = [jax.ShapeDtypeStruct((b, l, w), BF16)] * 2
        out_specs = [tok(fwd), tok(bwd)]
    else:
        out_shape = [jax.ShapeDtypeStruct((b, REC_HEADS, HEAD_DIM, HEAD_DIM), F32)] * 2
        out_specs = [state, state]
    state_scratch = pltpu.VMEM((REC_HEADS, HEAD_DIM, HEAD_DIM), F32)
    return pl.pallas_call(
        functools.partial(_rec_kernel, emit_out=emit_out),
        out_shape=out_shape,
        grid=(b, n),
        in_specs=[tok(fwd), tok(fwd), tok(fwd), tok(bwd), tok(bwd), tok(bwd), state, state,
                  const2(wf), const3(mf), const2(wb), const3(mb)],
        out_specs=out_specs,
        scratch_shapes=[state_scratch, state_scratch],
        compiler_params=pltpu.CompilerParams(
            dimension_semantics=("parallel", "arbitrary"), vmem_limit_bytes=VMEM_LIMIT),
        name="rec" if emit_out else "rec_ctx",
    )(q, v, lcf, q, v, lcb, s0f, s0b, wf, mf, wb, mb)


def _channel_dft_tables(scale):
    k = np.arange(GROUP_DIM)
    a = 2.0 * np.pi * np.outer(k, k) / GROUP_DIM
    return _bf16_table(np.cos(a) * scale), _bf16_table(np.sin(a) * scale)


def _outproj_kernel(xs_ref, of_ref, ob_ref, sg_ref, x_ref, mod_ref, grec_ref, cc_ref, sc_ref,
                    wf_ref, wo_ref, gffn_ref, wr_ref, x1_ref, h2_ref, aff_ref, *, n_experts):
    d = x_ref.shape[2]
    half = xs_ref.shape[2] // 2
    row = pl.program_id(0)
    gate1 = mod_ref[pl.ds(row, 1), 2 * d:3 * d]
    shift2 = mod_ref[pl.ds(row, 1), 3 * d:4 * d]
    scale2 = mod_ref[pl.ds(row, 1), 4 * d:5 * d]

    rows_per_group = x_ref.shape[1] // OUTPROJ_ROW_GROUPS
    for grp in range(OUTPROJ_ROW_GROUPS):
        rows = slice(grp * rows_per_group, (grp + 1) * rows_per_group)
        parts = []
        for g in range(FOURIER_GROUPS):
            lo = g * GROUP_DIM
            xre = xs_ref[0, rows, lo:lo + GROUP_DIM].astype(BF16)
            xim = xs_ref[0, rows, half + lo:half + lo + GROUP_DIM].astype(BF16)
            rez = _dot(xre, cc_ref[...]) + _dot(xim, sc_ref[...])
            parts.append(_dot(rez.astype(BF16), wf_ref[g]).astype(BF16))
        o = of_ref[0, rows, :].astype(F32) + ob_ref[0, rows, :].astype(F32)
        for h in range(REC_HEADS):
            lo = h * HEAD_DIM
            oh = _rms(o[:, lo:lo + HEAD_DIM]) * grec_ref[:, lo:lo + HEAD_DIM]
            parts.append((oh * sg_ref[0, rows, lo:lo + HEAD_DIM]).astype(BF16))
        mix = _dot(jnp.concatenate(parts, axis=1), wo_ref[...])
        x1 = x_ref[0, rows, :] + gate1 * mix
        x1_ref[0, rows, :] = x1

        h2 = _rms(x1) * gffn_ref[...] * (1.0 + scale2) + shift2
        h2_ref[0, rows, :] = h2
        logits = _dot3(h2, wr_ref[...])
        lane = lax.broadcasted_iota(I32, logits.shape, 1)
        logits = jnp.where(lane < n_experts, logits, -1e30)
        e = jnp.exp(logits - jnp.max(logits, axis=-1, keepdims=True))
        aff = e / jnp.sum(e, axis=-1, keepdims=True)
        aff_ref[0, :, rows] = aff.T[:n_experts]


def _outproj(xs, o_f, o_b, sg, x, mod, g_rec, w_fourier_bf16, w_out_bf16, g_ffn, w_router_pad, *, n_experts, tm):
    b, l, d = x.shape
    cc, sc = _channel_dft_tables(1.0 / np.sqrt(l * GROUP_DIM))
    tokens = lambda w: pl.BlockSpec((1, tm, w), lambda i, j: (i, j, 0))
    full = lambda a: pl.BlockSpec(a.shape, lambda i, j: (0,) * a.ndim)
    g_rec2 = g_rec.reshape(1, -1)
    g_ffn2 = g_ffn.reshape(1, d)
    return pl.pallas_call(
        functools.partial(_outproj_kernel, n_experts=n_experts),
        out_shape=[jax.ShapeDtypeStruct((b, l, d), F32),
                   jax.ShapeDtypeStruct((b, l, d), F32),
                   jax.ShapeDtypeStruct((b, n_experts, l), F32)],
        grid=(b, l // tm),
        in_specs=[tokens(xs.shape[2]), tokens(o_f.shape[2]), tokens(o_b.shape[2]), tokens(sg.shape[2]),
                  tokens(d), full(mod), full(g_rec2), full(cc), full(sc), full(w_fourier_bf16),
                  full(w_out_bf16), full(g_ffn2), full(w_router_pad)],
        out_specs=[tokens(d),
                   tokens(d),
                   pl.BlockSpec((1, n_experts, tm), lambda i, j: (i, 0, j))],
        compiler_params=pltpu.CompilerParams(
            dimension_semantics=("parallel", "parallel"), vmem_limit_bytes=VMEM_LIMIT),
        name="outproj",
    )(xs, o_f, o_b, sg, x, mod, g_rec2, cc, sc, w_fourier_bf16, w_out_bf16, g_ffn2, w_router_pad)


def _route_kernel(aff_ref, tri_ref, idx_ref, cum_ref, *, cap):
    a = aff_ref[0]
    ne, l = a.shape

    def value_bit(i, thr):
        cand = thr | jnp.left_shift(jnp.int32(1), 30 - i)
        cnt = jnp.sum(jnp.where(a >= pltpu.bitcast(cand, F32), 1.0, 0.0), axis=1, keepdims=True)
        return jnp.where(cnt >= cap, cand, thr)

    thr = lax.fori_loop(0, 31, value_bit, jnp.zeros((ne, 1), I32))
    subnormal = thr < MIN_NORMAL_BITS
    lo = jnp.where(subnormal, 0, thr)
    hi = jnp.where(subnormal, MIN_NORMAL_BITS, thr + 1)
    gt = a >= pltpu.bitcast(hi, F32)
    eq = (a >= pltpu.bitcast(lo, F32)) & jnp.logical_not(gt)
    need = cap - jnp.sum(jnp.where(gt, 1.0, 0.0), axis=1, keepdims=True)
    pos = lax.broadcasted_iota(I32, (ne, l), 1)
    nbits = int(l).bit_length()

    def index_bit(i, x):
        cand = x + jnp.left_shift(jnp.int32(1), nbits - 1 - i)
        cnt = jnp.sum(jnp.where(eq & (pos < cand), 1.0, 0.0), axis=1, keepdims=True)
        return jnp.where(cnt < need, cand, x)

    last = lax.fori_loop(0, nbits, index_bit, jnp.zeros((ne, 1), I32))
    sel = (gt | (eq & (pos <= last))).astype(BF16)

    n_tiles = l // LANE
    cum_ref[:, n_tiles:, :] = jnp.zeros((ne, LANE - n_tiles, LANE), F32)
    lane = lax.broadcasted_iota(I32, (ne, LANE), 1)
    off = jnp.zeros((ne, 1), F32)
    ends = jnp.zeros((ne, LANE), F32)
    for c in range(n_tiles):
        p = _dot(sel[:, c * LANE:(c + 1) * LANE], tri_ref[...]) + off
        cum_ref[:, c, :] = p
        off = p[:, LANE - 1:LANE]
        ends = jnp.where(lane == c, off, ends)
    never = float(2 * l)
    ends = jnp.where(lane < n_tiles, ends, never)
    starts = jnp.where(lane == 0, 0.0, pltpu.roll(ends, 1, 1))
    ones = jnp.ones((8, LANE), BF16)
    slot = lax.broadcasted_iota(I32, (cap, LANE), 0).astype(F32)
    for e in range(ne):
        en = ends[e:e + 1, :]
        st = starts[e:e + 1, :]
        before = en <= slot
        holds = ((st <= slot) & (slot < en)).astype(BF16)
        hi, lo = _split(cum_ref[e])
        tile_cum = _dot(holds, hi) + _dot(holds, lo)
        counts = jnp.where(before, float(LANE), 0.0) + jnp.where(tile_cum <= slot, 1.0, 0.0)
        tok = _dot_nt(ones, counts.astype(BF16))
        idx_ref[0, e:e + 1, :] = tok[0:1].astype(I32)


def _route(aff, cap):
    b, ne, l = aff.shape
    assert l % LANE == 0 and l // LANE <= LANE
    tri = jnp.asarray(np.triu(np.ones((LANE, LANE))), BF16)
    return pl.pallas_call(
        functools.partial(_route_kernel, cap=cap),
        out_shape=jax.ShapeDtypeStruct((b, ne, cap), I32),
        grid=(b,),
        in_specs=[pl.BlockSpec((1, ne, l), lambda i: (i, 0, 0)),
                  pl.BlockSpec(tri.shape, lambda i: (0, 0))],
        out_specs=pl.BlockSpec((1, ne, cap), lambda i: (i, 0, 0)),
        scratch_shapes=[pltpu.VMEM((ne, LANE, LANE), F32)],
        compiler_params=pltpu.CompilerParams(
            dimension_semantics=("parallel",), vmem_limit_bytes=VMEM_LIMIT),
        name="route",
    )(aff, tri)


def _ffn_kernel(src_ref, h_hbm, wg_ref, wu_ref, wd_ref, y_ref, xg_ref, xs_ref, sem, *, ne, nb, nf):
    e = pl.program_id(0)
    f = pl.program_id(1)
    n_used = xs_ref.shape[0]
    rows_per_step = xg_ref.shape[2]
    n_rows = rows_per_step * nf
    slot = e % 2

    def start_row(expert, slot_, step, i):
        t = src_ref[expert * n_rows + step * rows_per_step + i]
        pltpu.make_async_copy(h_hbm.at[pl.ds(t, 1)], xg_ref.at[slot_, step, pl.ds(i, 1)], sem.at[slot_]).start()

    def wait_rows(slot_):
        for step in range(nf):
            pltpu.make_async_copy(h_hbm.at[pl.ds(0, rows_per_step)], xg_ref.at[slot_, step], sem.at[slot_]).wait()

    @pl.when((e == 0) & (f == 0))
    def _():
        for step in range(nf):
            def issue(i, carry, step=step):
                start_row(0, 0, step, i)
                return carry

            lax.fori_loop(0, rows_per_step, issue, 0)

    @pl.when(f == 0)
    def _():
        wait_rows(slot)
        for step in range(nf):
            lo = step * rows_per_step
            cnt = min(rows_per_step, n_used - lo)
            if cnt > 0:
                xs_ref[lo:lo + cnt, :] = xg_ref[slot, step, 0:cnt, :].astype(BF16)
        y_ref[...] = jnp.zeros(y_ref.shape, F32)

    nxt = jnp.minimum(e + 1, ne - 1)
    for i in range(rows_per_step):
        start_row(nxt, 1 - slot, f, i)

    cap = n_used // nb

    wg = wg_ref[0].astype(BF16)
    wu = wu_ref[0].astype(BF16)
    wd = wd_ref[0].astype(BF16)
    for gb in range(nb):
        xs = xs_ref[gb * cap:(gb + 1) * cap, :]
        hid = _silu(_dot(xs, wg)) * _dot(xs, wu)
        y_ref[gb, 0] += _dot(hid.astype(BF16), wd)

    @pl.when((e == ne - 1) & (f == nf - 1))
    def _():
        wait_rows(1 - slot)


def _ffn(idx, h_rows, w_gate, w_up, w_down, *, tf):
    batch, _, cap = idx.shape
    ne, d, dff = w_gate.shape
    n_tokens = h_rows.shape[0] // batch
    nf = dff // tf
    rows_per_step = -(-(batch * cap) // (8 * nf)) * 8
    n_rows = rows_per_step * nf
    src = idx + (jnp.arange(batch, dtype=I32) * n_tokens)[:, None, None]
    src = jnp.transpose(src, (1, 0, 2)).reshape(ne, batch * cap)
    src = jnp.pad(src, ((0, 0), (0, n_rows - batch * cap)), mode="edge").reshape(-1)
    return pl.pallas_call(
        functools.partial(_ffn_kernel, ne=ne, nb=batch, nf=nf),
        out_shape=jax.ShapeDtypeStruct((batch, ne, cap, d), F32),
        grid_spec=pltpu.PrefetchScalarGridSpec(
            num_scalar_prefetch=1,
            grid=(ne, nf),
            in_specs=[pl.BlockSpec(memory_space=pl.ANY),
                      pl.BlockSpec((1, d, tf), lambda e, f, idx: (e, 0, f)),
                      pl.BlockSpec((1, d, tf), lambda e, f, idx: (e, 0, f)),
                      pl.BlockSpec((1, tf, d), lambda e, f, idx: (e, f, 0))],
            out_specs=pl.BlockSpec((batch, 1, cap, d), lambda e, f, idx: (0, e, 0, 0)),
            scratch_shapes=[pltpu.VMEM((2, nf, rows_per_step, d), F32),
                            pltpu.VMEM((batch * cap, d), BF16),
                            pltpu.SemaphoreType.DMA((2,))],
        ),
        compiler_params=pltpu.CompilerParams(
            dimension_semantics=("arbitrary", "arbitrary"), vmem_limit_bytes=VMEM_LIMIT),
        name="ffn",
    )(src, h_rows, w_gate, w_up, w_down)


COMBINE_GROUP = 8


def _combine_kernel(idx_ref, aff_ref, y_ref, o_hbm, acc_ref, sem):
    b = pl.program_id(0)
    e = pl.program_id(1)
    ne = pl.num_programs(1)
    cap = y_ref.shape[2]
    base = (b * ne + e) * cap

    @pl.when(e == 0)
    def _():
        acc_ref[...] = jnp.zeros(acc_ref.shape, F32)

    def add_rows(g, carry):
        j0 = pl.multiple_of(g * COMBINE_GROUP, COMBINE_GROUP)
        for u in range(COMBINE_GROUP):
            t = idx_ref[base + j0 + u]
            acc_ref[pl.ds(t, 1), :] += y_ref[0, 0, pl.ds(j0 + u, 1), :] * aff_ref[0, 0, t]
        return carry

    lax.fori_loop(0, cap // COMBINE_GROUP, add_rows, 0)

    @pl.when(e == ne - 1)
    def _():
        out = pltpu.make_async_copy(acc_ref, o_hbm.at[b], sem)
        out.start()
        out.wait()


def _combine(idx_flat, aff_rows, ye, n_tokens):
    b, ne, cap, d = ye.shape
    assert cap % COMBINE_GROUP == 0
    return pl.pallas_call(
        _combine_kernel,
        out_shape=jax.ShapeDtypeStruct((b, n_tokens, d), F32),
        grid_spec=pltpu.PrefetchScalarGridSpec(
            num_scalar_prefetch=1,
            grid=(b, ne),
            in_specs=[pl.BlockSpec((1, 1, n_tokens), lambda i, e, idx: (i * ne + e, 0, 0),
                                   memory_space=pltpu.SMEM),
                      pl.BlockSpec((1, 1, cap, d), lambda i, e, idx: (i, e, 0, 0))],
            out_specs=pl.BlockSpec(memory_space=pl.ANY),
            scratch_shapes=[pltpu.VMEM((n_tokens, d), F32), pltpu.SemaphoreType.DMA],
        ),
        compiler_params=pltpu.CompilerParams(
            dimension_semantics=("arbitrary", "arbitrary"), vmem_limit_bytes=VMEM_LIMIT),
        name="combine",
    )(idx_flat, aff_rows, ye)


def _final_kernel(x1_ref, moe_ref, mod_ref, g_ref, o_ref):
    d = x1_ref.shape[2]
    gate2 = mod_ref[pl.ds(pl.program_id(0), 1), 5 * d:6 * d]
    o_ref[0] = _rms(x1_ref[0] + gate2 * moe_ref[0]) * g_ref[...]


def _final(x1, moe, mod, g_final, tm):
    b, l, d = x1.shape
    tokens = pl.BlockSpec((1, tm, d), lambda i, j: (i, j, 0))
    return pl.pallas_call(
        _final_kernel,
        out_shape=jax.ShapeDtypeStruct((b, l, d), F32),
        grid=(b, l // tm),
        in_specs=[tokens, tokens, pl.BlockSpec(mod.shape, lambda i, j: (0, 0)),
                  pl.BlockSpec((1, d), lambda i, j: (0, 0))],
        out_specs=tokens,
        compiler_params=pltpu.CompilerParams(
            dimension_semantics=("parallel", "parallel"), vmem_limit_bytes=VMEM_LIMIT),
        name="final",
    )(x1, moe, mod, g_final.reshape(1, d))


def kernel(x, c, ctx, c_ctx, w_ada, b_ada, g_mix, w_in, w_fourier, lb_logits, g_rec, w_out, g_ffn, w_router, w_exp_gate, w_exp_up, w_exp_down, g_final):
    b, l, d = x.shape
    depth = w_ada.shape[0]
    assert depth == 1 and lb_logits.shape[0] == 2 and l == DFT_ROWS * DFT_COLS
    d_f = FOURIER_GROUPS * GROUP_DIM
    d_r = REC_HEADS * HEAD_DIM
    ne = w_router.shape[2]
    cap = EC_CAPACITY_FACTOR * l // ne

    cond = jnp.concatenate([c, c_ctx[None], jnp.zeros((8 - b - 1, d), F32)], axis=0)
    mod = _ada(cond, w_ada[0], b_ada[0])

    w_in_b = w_in[0].astype(BF16)
    col = lambda i: d_f + i * d_r
    lat_parts = (("raw", 0, F32), ("silu", col(0), BF16), ("raw", col(1), BF16), ("logf_fwd", col(2), F32),
                 ("logf_bwd", col(3), F32), ("silu", col(4), BF16))
    u, q, v, lcf, lcb, sg = _inproj(x, mod, g_mix[0], w_in_b, lb_logits, ctx_row=None, parts=lat_parts, tm=512)
    q_c, v_c, lcf_c, lcb_c = _inproj(ctx, mod, g_mix[0], w_in_b, lb_logits, ctx_row=b, parts=lat_parts[1:5],
                                     tm=ctx.shape[1])

    f1, twc, tws, f3a, f3b = _dft_tables()
    t4 = _dft1(u.reshape(b, DFT_ROWS, DFT_COLS, d_f), f1, twc, tws)
    xs = _dft3(t4, f3a, f3b).reshape(b, l, 2 * d_f)

    zero = jnp.zeros((b, REC_HEADS, HEAD_DIM, HEAD_DIM), F32)
    s_f, s_b = _rec(q_c, v_c, lcf_c, lcb_c, zero, zero, emit_out=False)
    o_f, o_b = _rec(q, v, lcf, lcb, s_f, s_b, emit_out=True)

    w_router_pad = jnp.pad(w_router[0], ((0, 0), (0, LANE - ne)))
    x1, h2, aff = _outproj(xs, o_f, o_b, sg, x, mod, g_rec[0], w_fourier[0].astype(BF16),
                           w_out[0].astype(BF16), g_ffn[0], w_router_pad, n_experts=ne, tm=512)

    idx = _route(aff, cap)
    ye = _ffn(idx, h2.reshape(b * l, d), w_exp_gate[0], w_exp_up[0], w_exp_down[0], tf=256)
    moe = _combine(idx.reshape(-1), aff.reshape(b * ne, 1, l), ye, l)
    return _final(x1, moe, mod, g_final, 512)
```

```python
import functools

import numpy as np
import jax
import jax.numpy as jnp
from jax import lax
from jax.experimental import pallas as pl
from jax.experimental.pallas import tpu as pltpu

F32 = jnp.float32
BF16 = jnp.bfloat16
I32 = jnp.int32

NORM_EPS = 1e-6
LOG2_E = 1.4426950408889634
MIN_NORMAL_BITS = 0x00800000
LANE = 128
N_ADA = 6
FOURIER_GROUPS = 4
GROUP_DIM = 128
REC_HEADS = 4
HEAD_DIM = 128
N_EXPERTS = 16
EC_CAPACITY_FACTOR = 2
DFT_ROWS = 128
DFT_COLS = 64
REC_CHUNK = 256
OUTPROJ_ROW_GROUPS = 2
VMEM_LIMIT = 56 * 1024 * 1024


def _dot(a, b):
    return jnp.dot(a, b, preferred_element_type=F32)


def _dot_nt(a, b):
    return lax.dot_general(a, b, (((1,), (1,)), ((), ())), preferred_element_type=F32)


def _dot_tn(a, b):
    return lax.dot_general(a, b, (((0,), (0,)), ((), ())), preferred_element_type=F32)


def _split(x):
    hi = x.astype(BF16)
    lo = (x - hi.astype(F32)).astype(BF16)
    return hi, lo


def _dot3(a, b):
    a_hi, a_lo = _split(a)
    b_hi, b_lo = _split(b)
    return _dot(a_hi, b_hi) + _dot(a_lo, b_hi) + _dot(a_hi, b_lo)


def _silu(x):
    return x * jax.nn.sigmoid(x)


def _rms(x):
    return x * lax.rsqrt(jnp.mean(x * x, axis=-1, keepdims=True) + NORM_EPS)


def _bf16_table(a):
    return jnp.asarray(a, F32).astype(BF16)


def _ada_kernel(cond_ref, w_ref, b_ref, o_ref):
    o_ref[...] = _dot3(_silu(cond_ref[...]), w_ref[...]) + b_ref[...]


def _ada(cond, w, b):
    d = cond.shape[1]
    n = w.shape[1]
    return pl.pallas_call(
        _ada_kernel,
        out_shape=jax.ShapeDtypeStruct((cond.shape[0], n), F32),
        grid=(n // d,),
        in_specs=[pl.BlockSpec(cond.shape, lambda j: (0, 0)),
                  pl.BlockSpec((d, d), lambda j: (0, j)),
                  pl.BlockSpec((1, d), lambda j: (0, j))],
        out_specs=pl.BlockSpec((cond.shape[0], d), lambda j: (0, j)),
        compiler_params=pltpu.CompilerParams(vmem_limit_bytes=VMEM_LIMIT),
        name="ada",
    )(cond, w, b.reshape(1, n))


def _inproj_kernel(x_ref, mod_ref, gmix_ref, w_ref, lbl_ref, *out_refs, ctx_row, parts):
    d = x_ref.shape[2]
    row = ctx_row if ctx_row is not None else pl.program_id(0)
    shift = mod_ref[pl.ds(row, 1), 0:d]
    scale = mod_ref[pl.ds(row, 1), d:2 * d]
    h = _rms(x_ref[0]) * gmix_ref[...] * (1.0 + scale) + shift
    hb = h.astype(BF16)

    l0 = lbl_ref[0]
    l1 = lbl_ref[1]
    m = jnp.maximum(l0, l1)
    e0 = jnp.exp(l0 - m)
    e1 = jnp.exp(l1 - m)
    lb = e0 / (e0 + e1)

    for o_ref, (kind, col, _) in zip(out_refs, parts):
        w = o_ref.shape[2]
        p = _dot(hb, w_ref[:, col:col + w])
        if kind == "raw":
            o_ref[0] = p.astype(o_ref.dtype)
        elif kind == "silu":
            o_ref[0] = _silu(p).astype(o_ref.dtype)
        else:
            lbd = lb[0:1] if kind == "logf_fwd" else lb[1:2]
            o_ref[0] = jnp.log(lbd + (1.0 - lbd) * jax.nn.sigmoid(p))


def _inproj(x, mod, g_mix, w_bf16, lb_logits, *, ctx_row, parts, tm):
    b, l, d = x.shape
    width = 512
    kern = functools.partial(_inproj_kernel, ctx_row=ctx_row, parts=parts)
    return pl.pallas_call(
        kern,
        out_shape=[jax.ShapeDtypeStruct((b, l, width), dt) for _, _, dt in parts],
        grid=(b, l // tm),
        in_specs=[pl.BlockSpec((1, tm, d), lambda i, j: (i, j, 0)),
                  pl.BlockSpec(mod.shape, lambda i, j: (0, 0)),
                  pl.BlockSpec((1, d), lambda i, j: (0, 0)),
                  pl.BlockSpec(w_bf16.shape, lambda i, j: (0, 0)),
                  pl.BlockSpec(lb_logits.shape, lambda i, j: (0, 0, 0))],
        out_specs=[pl.BlockSpec((1, tm, width), lambda i, j: (i, j, 0)) for _ in parts],
        compiler_params=pltpu.CompilerParams(
            dimension_semantics=("parallel", "parallel"), vmem_limit_bytes=VMEM_LIMIT),
        name="inproj_ctx" if ctx_row is not None else "inproj",
    )(x, mod, g_mix.reshape(1, d), w_bf16, lb_logits)


def _dft_tables():
    n1, n2 = DFT_ROWS, DFT_COLS
    k1 = np.arange(n1)
    a1 = 2.0 * np.pi * np.outer(k1, k1) / n1
    f1 = np.concatenate([np.cos(a1), -np.sin(a1)], axis=0)
    c = np.arange(n2)
    th = 2.0 * np.pi * np.outer(c, k1) / (n1 * n2)
    twc = np.repeat(np.cos(th)[:, :, None], LANE, axis=2)
    tws = np.repeat(np.sin(th)[:, :, None], LANE, axis=2)
    a2 = 2.0 * np.pi * np.outer(c, c) / n2
    f3a = np.concatenate([np.cos(a2), -np.sin(a2)], axis=0)
    f3b = np.concatenate([np.sin(a2), np.cos(a2)], axis=0)
    return (_bf16_table(f1), jnp.asarray(twc, F32), jnp.asarray(tws, F32),
            _bf16_table(f3a), _bf16_table(f3b))


def _dft1_kernel(*refs):
    n_slab = (len(refs) - 3) // 3
    x_refs = refs[:n_slab]
    f1_ref, twc_ref, tws_ref = refs[n_slab:n_slab + 3]
    re_refs = refs[n_slab + 3:2 * n_slab + 3]
    im_refs = refs[2 * n_slab + 3:]
    n1, cb = x_refs[0].shape[1:3]
    flat = lambda ref: ref.at[0].reshape(n1 * cb, LANE)
    xs, res, ims = [[flat(r) for r in group] for group in (x_refs, re_refs, im_refs)]
    for cc in range(cb):
        rows = pl.ds(cc, n1, stride=cb)
        xc = jnp.concatenate([x[rows, :] for x in xs], axis=1).astype(BF16)
        g = _dot(f1_ref[...], xc)
        tre = g[:n1]
        tim = g[n1:]
        cw = twc_ref[cc]
        sw = tws_ref[cc]
        for s in range(n_slab):
            cols = slice(s * LANE, (s + 1) * LANE)
            res[s][rows, :] = tre[:, cols] * cw + tim[:, cols] * sw
            ims[s][rows, :] = tim[:, cols] * cw - tre[:, cols] * sw


def _dft1(u4, f1, twc, tws):
    b, n1, n2, ch = u4.shape
    cb = 8
    n_slab = ch // LANE
    slab = lambda s: pl.BlockSpec((1, n1, cb, LANE), lambda i, j, s=s: (i, 0, j, s))
    return pl.pallas_call(
        _dft1_kernel,
        out_shape=[jax.ShapeDtypeStruct((b, n1, n2, LANE), F32)] * (2 * n_slab),
        grid=(b, n2 // cb),
        in_specs=[slab(s) for s in range(n_slab)] + [
            pl.BlockSpec(f1.shape, lambda i, j: (0, 0)),
            pl.BlockSpec((cb, n1, LANE), lambda i, j: (j, 0, 0)),
            pl.BlockSpec((cb, n1, LANE), lambda i, j: (j, 0, 0))],
        out_specs=[pl.BlockSpec((1, n1, cb, LANE), lambda i, j: (i, 0, j, 0))] * (2 * n_slab),
        compiler_params=pltpu.CompilerParams(
            dimension_semantics=("parallel", "parallel"), vmem_limit_bytes=VMEM_LIMIT),
        name="dft1",
    )(*([u4] * n_slab), f1, twc, tws)


def _dft3_kernel(*refs):
    n_slab = (len(refs) - 3) // 2
    re_refs = refs[:n_slab]
    im_refs = refs[n_slab:2 * n_slab]
    f3a_ref, f3b_ref, o_ref = refs[2 * n_slab:]
    n2 = re_refs[0].shape[2]
    ch = n_slab * LANE
    for kk in range(re_refs[0].shape[1]):
        tre = jnp.concatenate([r[0, kk] for r in re_refs], axis=1).astype(BF16)
        tim = jnp.concatenate([r[0, kk] for r in im_refs], axis=1).astype(BF16)
        x = _dot(f3a_ref[...], tre) + _dot(f3b_ref[...], tim)
        o_ref[0, :, kk, 0:ch] = x[:n2]
        o_ref[0, :, kk, ch:2 * ch] = x[n2:]


def _dft3(t_slabs, f3a, f3b):
    n_slab = len(t_slabs) // 2
    b, n1, n2, _ = t_slabs[0].shape
    w = 2 * n_slab * LANE
    kb = 8
    return pl.pallas_call(
        _dft3_kernel,
        out_shape=jax.ShapeDtypeStruct((b, n2, n1, w), F32),
        grid=(b, n1 // kb),
        in_specs=[pl.BlockSpec((1, kb, n2, LANE), lambda i, j: (i, j, 0, 0))] * (2 * n_slab) + [
            pl.BlockSpec(f3a.shape, lambda i, j: (0, 0)),
            pl.BlockSpec(f3b.shape, lambda i, j: (0, 0))],
        out_specs=pl.BlockSpec((1, n2, kb, w), lambda i, j: (i, 0, j, 0)),
        compiler_params=pltpu.CompilerParams(
            dimension_semantics=("parallel", "parallel"), vmem_limit_bytes=VMEM_LIMIT),
        name="dft3",
    )(*t_slabs, f3a, f3b)


def _rec_tables(reverse):
    c = REC_CHUNK
    t = np.arange(c)[:, None]
    s = np.arange(c)[None, :]
    masks = [np.eye(c)]
    m = 1
    while m < c:
        masks.append(((t // (2 * m) == s // (2 * m)) & ((t // m) % 2 == 1) & ((s // m) % 2 == 0)).astype(np.float64))
        m *= 2
    m = 4
    while m < c:
        masks.append(np.where((t // m) % 2 == 1, 1.0, -1.0) * np.ones((1, c)))
        m *= 2
    cum = (s <= t).astype(np.float64)
    if reverse:
        cum = cum[::-1, ::-1]
        masks = [a[::-1, ::-1] for a in masks]
    return jnp.asarray(cum, BF16), jnp.asarray(np.stack(masks), F32)


def _level_logdecay(b, lc, m, reverse, mask_ref):
    c, d = lc.shape
    if m <= 2:
        row = lax.broadcasted_iota(I32, (c, d), 0)
        if m == 1:
            return jnp.where((row & 1) == (0 if reverse else 1), lc, 0.0)
        r = row & 3
        nxt = pltpu.roll(lc, c - 1, 0)
        prv = pltpu.roll(lc, 1, 0)
        if reverse:
            return jnp.where(r == 0, lc + nxt, jnp.where(r == 1, lc, jnp.where(r == 2, 0.0, prv)))
        return jnp.where(r == 0, nxt, jnp.where(r == 1, 0.0, jnp.where(r == 2, lc, lc + prv)))
    anchor_row = m if reverse else m - 1
    b3 = b.reshape(c // (2 * m), 2 * m, d)
    anchor = jnp.broadcast_to(b3[:, anchor_row:anchor_row + 1, :], b3.shape).reshape(c, d)
    n_pair_planes = 1 + (c.bit_length() - 1)
    sign = mask_ref[n_pair_planes + (m.bit_length() - 1) - 2, :, 0:d]
    return (b - anchor) * sign


def _rec_direction(q, v, lc, st_ref, h, cum_ref, mask_ref, emit_out, reverse):
    c = REC_CHUNK
    lc = lc * LOG2_E
    k = 1.0 - jnp.exp2(lc)
    hi, lo = _split(lc)
    res = _dot(cum_ref[...], jnp.concatenate([hi, lo], axis=1))
    b = res[:, :HEAD_DIM] + res[:, HEAD_DIM:]
    last = 0 if reverse else c - 1
    b_last = b[last:last + 1, :]
    st = st_ref[h]
    kb = k.astype(BF16)
    o = None
    if emit_out:
        half = c // 2
        lo_, hi_ = slice(0, half), slice(half, c)
        p = _dot_nt(q, kb)
        a_lo = mask_ref[0, lo_, lo_] * p[lo_, lo_]
        a_hi = mask_ref[0, hi_, hi_] * p[hi_, hi_]
        m, i = 1, 1
        while m < half:
            dm = jnp.exp2(_level_logdecay(b, lc, m, reverse, mask_ref)).astype(BF16)
            p = _dot_nt(q * dm, kb * dm)
            a_lo = a_lo + mask_ref[i, lo_, lo_] * p[lo_, lo_]
            a_hi = a_hi + mask_ref[i, hi_, hi_] * p[hi_, hi_]
            m, i = 2 * m, i + 1
        dm = jnp.exp2(_level_logdecay(b, lc, half, reverse, mask_ref)).astype(BF16)
        late, early = (lo_, hi_) if reverse else (hi_, lo_)
        cross = _dot_nt(q[late] * dm[late], kb[early] * dm[early])
        zero = jnp.zeros((half, half), F32)
        if reverse:
            a = jnp.concatenate([jnp.concatenate([a_lo, cross], axis=1),
                                 jnp.concatenate([zero, a_hi], axis=1)], axis=0)
        else:
            a = jnp.concatenate([jnp.concatenate([a_lo, zero], axis=1),
                                 jnp.concatenate([cross, a_hi], axis=1)], axis=0)
        qd = q * jnp.exp2(b).astype(BF16)
        o = _dot_nt(qd, st.astype(BF16)) + _dot(a.astype(BF16), v)
    kd = kb * jnp.exp2(b_last - b).astype(BF16)
    st_ref[h] = st * jnp.exp2(b_last) + _dot_tn(v, kd)
    return o


def _rec_kernel(qf_ref, vf_ref, lf_ref, qb_ref, vb_ref, lb_ref, s0f_ref, s0b_ref,
                wf_ref, mf_ref, wb_ref, mb_ref, of_ref, ob_ref, sf_ref, sb_ref, *, emit_out):
    n = pl.program_id(1)

    @pl.when(n == 0)
    def _():
        sf_ref[...] = s0f_ref[0]
        sb_ref[...] = s0b_ref[0]

    for h in range(REC_HEADS):
        cols = slice(h * HEAD_DIM, (h + 1) * HEAD_DIM)
        o_f = _rec_direction(qf_ref[0, :, cols], vf_ref[0, :, cols], lf_ref[0, :, cols],
                             sf_ref, h, wf_ref, mf_ref, emit_out, False)
        o_b = _rec_direction(qb_ref[0, :, cols], vb_ref[0, :, cols], lb_ref[0, :, cols],
                             sb_ref, h, wb_ref, mb_ref, emit_out, True)
        if emit_out:
            of_ref[0, :, cols] = o_f.astype(of_ref.dtype)
            ob_ref[0, :, cols] = o_b.astype(ob_ref.dtype)
    if not emit_out:
        @pl.when(n == pl.num_programs(1) - 1)
        def _():
            of_ref[0] = sf_ref[...]
            ob_ref[0] = sb_ref[...]


def _rec(q, v, lcf, lcb, s0f, s0b, *, emit_out):
    b, l, w = q.shape
    c = REC_CHUNK
    n = l // c
    wf, mf = _rec_tables(False)
    wb, mb = _rec_tables(True)
    fwd = lambda i, j: (i, j, 0)
    bwd = lambda i, j: (i, n - 1 - j, 0)
    tok = lambda im: pl.BlockSpec((1, c, w), im)
    state = pl.BlockSpec((1, REC_HEADS, HEAD_DIM, HEAD_DIM), lambda i, j: (i, 0, 0, 0))
    const2 = lambda a: pl.BlockSpec(a.shape, lambda i, j: (0, 0))
    const3 = lambda a: pl.BlockSpec(a.shape, lambda i, j: (0, 0, 0))
    if emit_out:
        out_shape = [jax.ShapeDtypeStruct((b, l, w), BF16)] * 2
        out_specs = [tok(fwd), tok(bwd)]
    else:
        out_shape = [jax.ShapeDtypeStruct((b, REC_HEADS, HEAD_DIM, HEAD_DIM), F32)] * 2
        out_specs = [state, state]
    state_scratch = pltpu.VMEM((REC_HEADS, HEAD_DIM, HEAD_DIM), F32)
    return pl.pallas_call(
        functools.partial(_rec_kernel, emit_out=emit_out),
        out_shape=out_shape,
        grid=(b, n),
        in_specs=[tok(fwd), tok(fwd), tok(fwd), tok(bwd), tok(bwd), tok(bwd), state, state,
                  const2(wf), const3(mf), const2(wb), const3(mb)],
        out_specs=out_specs,
        scratch_shapes=[state_scratch, state_scratch],
        compiler_params=pltpu.CompilerParams(
            dimension_semantics=("parallel", "arbitrary"), vmem_limit_bytes=VMEM_LIMIT),
        name="rec" if emit_out else "rec_ctx",
    )(q, v, lcf, q, v, lcb, s0f, s0b, wf, mf, wb, mb)


def _channel_dft_tables(scale):
    k = np.arange(GROUP_DIM)
    a = 2.0 * np.pi * np.outer(k, k) / GROUP_DIM
    return _bf16_table(np.cos(a) * scale), _bf16_table(np.sin(a) * scale)


def _outproj_kernel(xs_ref, of_ref, ob_ref, sg_ref, x_ref, mod_ref, grec_ref, cc_ref, sc_ref,
                    wf_ref, wo_ref, gffn_ref, wr_ref, x1_ref, h2_ref, aff_ref, *, n_experts):
    d = x_ref.shape[2]
    half = xs_ref.shape[2] // 2
    row = pl.program_id(0)
    gate1 = mod_ref[pl.ds(row, 1), 2 * d:3 * d]
    shift2 = mod_ref[pl.ds(row, 1), 3 * d:4 * d]
    scale2 = mod_ref[pl.ds(row, 1), 4 * d:5 * d]

    rows_per_group = x_ref.shape[1] // OUTPROJ_ROW_GROUPS
    for grp in range(OUTPROJ_ROW_GROUPS):
        rows = slice(grp * rows_per_group, (grp + 1) * rows_per_group)
        parts = []
        for g in range(FOURIER_GROUPS):
            lo = g * GROUP_DIM
            xre = xs_ref[0, rows, lo:lo + GROUP_DIM].astype(BF16)
            xim = xs_ref[0, rows, half + lo:half + lo + GROUP_DIM].astype(BF16)
            rez = _dot(xre, cc_ref[...]) + _dot(xim, sc_ref[...])
            parts.append(_dot(rez.astype(BF16), wf_ref[g]).astype(BF16))
        o = of_ref[0, rows, :].astype(F32) + ob_ref[0, rows, :].astype(F32)
        for h in range(REC_HEADS):
            lo = h * HEAD_DIM
            oh = _rms(o[:, lo:lo + HEAD_DIM]) * grec_ref[:, lo:lo + HEAD_DIM]
            parts.append((oh * sg_ref[0, rows, lo:lo + HEAD_DIM]).astype(BF16))
        mix = _dot(jnp.concatenate(parts, axis=1), wo_ref[...])
        x1 = x_ref[0, rows, :] + gate1 * mix
        x1_ref[0, rows, :] = x1

        h2 = _rms(x1) * gffn_ref[...] * (1.0 + scale2) + shift2
        h2_ref[0, rows, :] = h2
        logits = _dot3(h2, wr_ref[...])
        lane = lax.broadcasted_iota(I32, logits.shape, 1)
        logits = jnp.where(lane < n_experts, logits, -1e30)
        e = jnp.exp(logits - jnp.max(logits, axis=-1, keepdims=True))
        aff = e / jnp.sum(e, axis=-1, keepdims=True)
        aff_ref[0, :, rows] = aff.T[:n_experts]


def _outproj(xs, o_f, o_b, sg, x, mod, g_rec, w_fourier_bf16, w_out_bf16, g_ffn, w_router_pad, *, n_experts, tm):
    b, l, d = x.shape
    cc, sc = _channel_dft_tables(1.0 / np.sqrt(l * GROUP_DIM))
    tokens = lambda w: pl.BlockSpec((1, tm, w), lambda i, j: (i, j, 0))
    full = lambda a: pl.BlockSpec(a.shape, lambda i, j: (0,) * a.ndim)
    g_rec2 = g_rec.reshape(1, -1)
    g_ffn2 = g_ffn.reshape(1, d)
    return pl.pallas_call(
        functools.partial(_outproj_kernel, n_experts=n_experts),
        out_shape=[jax.ShapeDtypeStruct((b, l, d), F32),
                   jax.ShapeDtypeStruct((b, l, d), F32),
                   jax.ShapeDtypeStruct((b, n_experts, l), F32)],
        grid=(b, l // tm),
        in_specs=[tokens(xs.shape[2]), tokens(o_f.shape[2]), tokens(o_b.shape[2]), tokens(sg.shape[2]),
                  tokens(d), full(mod), full(g_rec2), full(cc), full(sc), full(w_fourier_bf16),
                  full(w_out_bf16), full(g_ffn2), full(w_router_pad)],
        out_specs=[tokens(d),
                   tokens(d),
                   pl.BlockSpec((1, n_experts, tm), lambda i, j: (i, 0, j))],
        compiler_params=pltpu.CompilerParams(
            dimension_semantics=("parallel", "parallel"), vmem_limit_bytes=VMEM_LIMIT),
        name="outproj",
    )(xs, o_f, o_b, sg, x, mod, g_rec2, cc, sc, w_fourier_bf16, w_out_bf16, g_ffn2, w_router_pad)


def _route_kernel(aff_ref, tri_ref, idx_ref, cum_ref, *, cap):
    a = aff_ref[0]
    ne, l = a.shape

    def value_bit(i, thr):
        cand = thr | jnp.left_shift(jnp.int32(1), 30 - i)
        cnt = jnp.sum(jnp.where(a >= pltpu.bitcast(cand, F32), 1.0, 0.0), axis=1, keepdims=True)
        return jnp.where(cnt >= cap, cand, thr)

    thr = lax.fori_loop(0, 31, value_bit, jnp.zeros((ne, 1), I32))
    subnormal = thr < MIN_NORMAL_BITS
    lo = jnp.where(subnormal, 0, thr)
    hi = jnp.where(subnormal, MIN_NORMAL_BITS, thr + 1)
    gt = a >= pltpu.bitcast(hi, F32)
    eq = (a >= pltpu.bitcast(lo, F32)) & jnp.logical_not(gt)
    need = cap - jnp.sum(jnp.where(gt, 1.0, 0.0), axis=1, keepdims=True)
    pos = lax.broadcasted_iota(I32, (ne, l), 1)
    nbits = int(l).bit_length()

    def index_bit(i, x):
        cand = x + jnp.left_shift(jnp.int32(1), nbits - 1 - i)
        cnt = jnp.sum(jnp.where(eq & (pos < cand), 1.0, 0.0), axis=1, keepdims=True)
        return jnp.where(cnt < need, cand, x)

    last = lax.fori_loop(0, nbits, index_bit, jnp.zeros((ne, 1), I32))
    sel = (gt | (eq & (pos <= last))).astype(BF16)

    n_tiles = l // LANE
    cum_ref[:, n_tiles:, :] = jnp.zeros((ne, LANE - n_tiles, LANE), F32)
    lane = lax.broadcasted_iota(I32, (ne, LANE), 1)
    off = jnp.zeros((ne, 1), F32)
    ends = jnp.zeros((ne, LANE), F32)
    for c in range(n_tiles):
        p = _dot(sel[:, c * LANE:(c + 1) * LANE], tri_ref[...]) + off
        cum_ref[:, c, :] = p
        off = p[:, LANE - 1:LANE]
        ends = jnp.where(lane == c, off, ends)
    never = float(2 * l)
    ends = jnp.where(lane < n_tiles, ends, never)
    starts = jnp.where(lane == 0, 0.0, pltpu.roll(ends, 1, 1))
    ones = jnp.ones((8, LANE), BF16)
    slot = lax.broadcasted_iota(I32, (cap, LANE), 0).astype(F32)
    for e in range(ne):
        en = ends[e:e + 1, :]
        st = starts[e:e + 1, :]
        before = en <= slot
        holds = ((st <= slot) & (slot < en)).astype(BF16)
        hi, lo = _split(cum_ref[e])
        tile_cum = _dot(holds, hi) + _dot(holds, lo)
        counts = jnp.where(before, float(LANE), 0.0) + jnp.where(tile_cum <= slot, 1.0, 0.0)
        tok = _dot_nt(ones, counts.astype(BF16))
        idx_ref[0, e:e + 1, :] = tok[0:1].astype(I32)


def _route(aff, cap):
    b, ne, l = aff.shape
    assert l % LANE == 0 and l // LANE <= LANE
    tri = jnp.asarray(np.triu(np.ones((LANE, LANE))), BF16)
    return pl.pallas_call(
        functools.partial(_route_kernel, cap=cap),
        out_shape=jax.ShapeDtypeStruct((b, ne, cap), I32),
        grid=(b,),
        in_specs=[pl.BlockSpec((1, ne, l), lambda i: (i, 0, 0)),
                  pl.BlockSpec(tri.shape, lambda i: (0, 0))],
        out_specs=pl.BlockSpec((1, ne, cap), lambda i: (i, 0, 0)),
        scratch_shapes=[pltpu.VMEM((ne, LANE, LANE), F32)],
        compiler_params=pltpu.CompilerParams(
            dimension_semantics=("parallel",), vmem_limit_bytes=VMEM_LIMIT),
        name="route",
    )(aff, tri)


def _ffn_kernel(src_ref, h_hbm, wg_ref, wu_ref, wd_ref, y_ref, xg_ref, xs_ref, sem, *, ne, nb, nf):
    e = pl.program_id(0)
    f = pl.program_id(1)
    n_used = xs_ref.shape[0]
    rows_per_step = xg_ref.shape[2]
    n_rows = rows_per_step * nf
    slot = e % 2

    def start_row(expert, slot_, step, i):
        t = src_ref[expert * n_rows + step * rows_per_step + i]
        pltpu.make_async_copy(h_hbm.at[pl.ds(t, 1)], xg_ref.at[slot_, step, pl.ds(i, 1)], sem.at[slot_]).start()

    def wait_rows(slot_):
        for step in range(nf):
            pltpu.make_async_copy(h_hbm.at[pl.ds(0, rows_per_step)], xg_ref.at[slot_, step], sem.at[slot_]).wait()

    @pl.when((e == 0) & (f == 0))
    def _():
        for step in range(nf):
            def issue(i, carry, step=step):
                start_row(0, 0, step, i)
                return carry

            lax.fori_loop(0, rows_per_step, issue, 0)

    @pl.when(f == 0)
    def _():
        wait_rows(slot)
        for step in range(nf):
            lo = step * rows_per_step
            cnt = min(rows_per_step, n_used - lo)
            if cnt > 0:
                xs_ref[lo:lo + cnt, :] = xg_ref[slot, step, 0:cnt, :].astype(BF16)
        y_ref[...] = jnp.zeros(y_ref.shape, F32)

    nxt = jnp.minimum(e + 1, ne - 1)
    for i in range(rows_per_step):
        start_row(nxt, 1 - slot, f, i)

    cap = n_used // nb

    wg = wg_ref[0].astype(BF16)
    wu = wu_ref[0].astype(BF16)
    wd = wd_ref[0].astype(BF16)
    for gb in range(nb):
        xs = xs_ref[gb * cap:(gb + 1) * cap, :]
        hid = _silu(_dot(xs, wg)) * _dot(xs, wu)
        y_ref[gb, 0] += _dot(hid.astype(BF16), wd)

    @pl.when((e == ne - 1) & (f == nf - 1))
    def _():
        wait_rows(1 - slot)


def _ffn(idx, h_rows, w_gate, w_up, w_down, *, tf):
    batch, _, cap = idx.shape
    ne, d, dff = w_gate.shape
    n_tokens = h_rows.shape[0] // batch
    nf = dff // tf
    rows_per_step = -(-(batch * cap) // (8 * nf)) * 8
    n_rows = rows_per_step * nf
    src = idx + (jnp.arange(batch, dtype=I32) * n_tokens)[:, None, None]
    src = jnp.transpose(src, (1, 0, 2)).reshape(ne, batch * cap)
    src = jnp.pad(src, ((0, 0), (0, n_rows - batch * cap)), mode="edge").reshape(-1)
    return pl.pallas_call(
        functools.partial(_ffn_kernel, ne=ne, nb=batch, nf=nf),
        out_shape=jax.ShapeDtypeStruct((batch, ne, cap, d), F32),
        grid_spec=pltpu.PrefetchScalarGridSpec(
            num_scalar_prefetch=1,
            grid=(ne, nf),
            in_specs=[pl.BlockSpec(memory_space=pl.ANY),
                      pl.BlockSpec((1, d, tf), lambda e, f, idx: (e, 0, f)),
                      pl.BlockSpec((1, d, tf), lambda e, f, idx: (e, 0, f)),
                      pl.BlockSpec((1, tf, d), lambda e, f, idx: (e, f, 0))],
            out_specs=pl.BlockSpec((batch, 1, cap, d), lambda e, f, idx: (0, e, 0, 0)),
            scratch_shapes=[pltpu.VMEM((2, nf, rows_per_step, d), F32),
                            pltpu.VMEM((batch * cap, d), BF16),
                            pltpu.SemaphoreType.DMA((2,))],
        ),
        compiler_params=pltpu.CompilerParams(
            dimension_semantics=("arbitrary", "arbitrary"), vmem_limit_bytes=VMEM_LIMIT),
        name="ffn",
    )(src, h_rows, w_gate, w_up, w_down)


COMBINE_GROUP = 8


def _combine_kernel(idx_ref, aff_ref, y_ref, o_hbm, acc_ref, sem):
    b = pl.program_id(0)
    e = pl.program_id(1)
    ne = pl.num_programs(1)
    cap = y_ref.shape[2]
    base = (b * ne + e) * cap

    @pl.when(e == 0)
    def _():
        acc_ref[...] = jnp.zeros(acc_ref.shape, F32)

    def add_rows(g, carry):
        j0 = pl.multiple_of(g * COMBINE_GROUP, COMBINE_GROUP)
        for u in range(COMBINE_GROUP):
            t = idx_ref[base + j0 + u]
            acc_ref[pl.ds(t, 1), :] += y_ref[0, 0, pl.ds(j0 + u, 1), :] * aff_ref[0, 0, t]
        return carry

    lax.fori_loop(0, cap // COMBINE_GROUP, add_rows, 0)

    @pl.when(e == ne - 1)
    def _():
        out = pltpu.make_async_copy(acc_ref, o_hbm.at[b], sem)
        out.start()
        out.wait()


def _combine(idx_flat, aff_rows, ye, n_tokens):
    b, ne, cap, d = ye.shape
    assert cap % COMBINE_GROUP == 0
    return pl.pallas_call(
        _combine_kernel,
        out_shape=jax.ShapeDtypeStruct((b, n_tokens, d), F32),
        grid_spec=pltpu.PrefetchScalarGridSpec(
            num_scalar_prefetch=1,
            grid=(b, ne),
            in_specs=[pl.BlockSpec((1, 1, n_tokens), lambda i, e, idx: (i * ne + e, 0, 0),
                                   memory_space=pltpu.SMEM),
                      pl.BlockSpec((1, 1, cap, d), lambda i, e, idx: (i, e, 0, 0))],
            out_specs=pl.BlockSpec(memory_space=pl.ANY),
            scratch_shapes=[pltpu.VMEM((n_tokens, d), F32), pltpu.SemaphoreType.DMA],
        ),
        compiler_params=pltpu.CompilerParams(
            dimension_semantics=("arbitrary", "arbitrary"), vmem_limit_bytes=VMEM_LIMIT),
        name="combine",
    )(idx_flat, aff_rows, ye)


def _final_kernel(x1_ref, moe_ref, mod_ref, g_ref, o_ref):
    d = x1_ref.shape[2]
    gate2 = mod_ref[pl.ds(pl.program_id(0), 1), 5 * d:6 * d]
    o_ref[0] = _rms(x1_ref[0] + gate2 * moe_ref[0]) * g_ref[...]


def _final(x1, moe, mod, g_final, tm):
    b, l, d = x1.shape
    tokens = pl.BlockSpec((1, tm, d), lambda i, j: (i, j, 0))
    return pl.pallas_call(
        _final_kernel,
        out_shape=jax.ShapeDtypeStruct((b, l, d), F32),
        grid=(b, l // tm),
        in_specs=[tokens, tokens, pl.BlockSpec(mod.shape, lambda i, j: (0, 0)),
                  pl.BlockSpec((1, d), lambda i, j: (0, 0))],
        out_specs=tokens,
        compiler_params=pltpu.CompilerParams(
            dimension_semantics=("parallel", "parallel"), vmem_limit_bytes=VMEM_LIMIT),
        name="final",
    )(x1, moe, mod, g_final.reshape(1, d))


def kernel(x, c, ctx, c_ctx, w_ada, b_ada, g_mix, w_in, w_fourier, lb_logits, g_rec, w_out, g_ffn, w_router, w_exp_gate, w_exp_up, w_exp_down, g_final):
    b, l, d = x.shape
    depth = w_ada.shape[0]
    assert depth == 1 and lb_logits.shape[0] == 2 and l == DFT_ROWS * DFT_COLS
    d_f = FOURIER_GROUPS * GROUP_DIM
    d_r = REC_HEADS * HEAD_DIM
    ne = w_router.shape[2]
    cap = EC_CAPACITY_FACTOR * l // ne

    cond = jnp.concatenate([c, c_ctx[None], jnp.zeros((8 - b - 1, d), F32)], axis=0)
    mod = _ada(cond, w_ada[0], b_ada[0])

    w_in_b = w_in[0].astype(BF16)
    col = lambda i: d_f + i * d_r
    lat_parts = (("raw", 0, F32), ("silu", col(0), BF16), ("raw", col(1), BF16), ("logf_fwd", col(2), F32),
                 ("logf_bwd", col(3), F32), ("silu", col(4), BF16))
    u, q, v, lcf, lcb, sg = _inproj(x, mod, g_mix[0], w_in_b, lb_logits, ctx_row=None, parts=lat_parts, tm=512)
    q_c, v_c, lcf_c, lcb_c = _inproj(ctx, mod, g_mix[0], w_in_b, lb_logits, ctx_row=b, parts=lat_parts[1:5],
                                     tm=ctx.shape[1])

    f1, twc, tws, f3a, f3b = _dft_tables()
    t4 = _dft1(u.reshape(b, DFT_ROWS, DFT_COLS, d_f), f1, twc, tws)
    xs = _dft3(t4, f3a, f3b).reshape(b, l, 2 * d_f)

    zero = jnp.zeros((b, REC_HEADS, HEAD_DIM, HEAD_DIM), F32)
    s_f, s_b = _rec(q_c, v_c, lcf_c, lcb_c, zero, zero, emit_out=False)
    o_f, o_b = _rec(q, v, lcf, lcb, s_f, s_b, emit_out=True)

    w_router_pad = jnp.pad(w_router[0], ((0, 0), (0, LANE - ne)))
    x1, h2, aff = _outproj(xs, o_f, o_b, sg, x, mod, g_rec[0], w_fourier[0].astype(BF16),
                           w_out[0].astype(BF16), g_ffn[0], w_router_pad, n_experts=ne, tm=512)

    idx = _route(aff, cap)
    ye = _ffn(idx, h2.reshape(b * l, d), w_exp_gate[0], w_exp_up[0], w_exp_down[0], tf=256)
    moe = _combine(idx.reshape(-1), aff.reshape(b * ne, 1, l), ye, l)
    return _final(x1, moe, mod, g_final, 512)
```

```python
import functools

import numpy as np
import jax
import jax.numpy as jnp
from jax import lax
from jax.experimental import pallas as pl
from jax.experimental.pallas import tpu as pltpu

F32 = jnp.float32
BF16 = jnp.bfloat16
I32 = jnp.int32

NORM_EPS = 1e-6
LOG2_E = 1.4426950408889634
MIN_NORMAL_BITS = 0x00800000
LANE = 128
SUBLANE = 8
N_ADA = 6
FOURIER_GROUPS = 4
GROUP_DIM = 128
REC_HEADS = 4
HEAD_DIM = 128
N_EXPERTS = 16
EC_CAPACITY_FACTOR = 2
DFT_ROWS = 128
DFT_COLS = 64
REC_CHUNK = 256
OUTPROJ_ROW_GROUPS = 2
INPROJ_ROW_GROUPS = 2
VMEM_LIMIT = 56 * 1024 * 1024


def _dot(a, b):
    return jnp.dot(a, b, preferred_element_type=F32)


def _dot_nt(a, b):
    return lax.dot_general(a, b, (((1,), (1,)), ((), ())), preferred_element_type=F32)


def _dot_tn(a, b):
    return lax.dot_general(a, b, (((0,), (0,)), ((), ())), preferred_element_type=F32)


def _split(x):
    hi = x.astype(BF16)
    lo = (x - hi.astype(F32)).astype(BF16)
    return hi, lo


def _dot3(a, b):
    a_hi, a_lo = _split(a)
    b_hi, b_lo = _split(b)
    return _dot(a_hi, b_hi) + _dot(a_lo, b_hi) + _dot(a_hi, b_lo)


def _silu(x):
    return x * jax.nn.sigmoid(x)


def _rms(x):
    return x * lax.rsqrt(jnp.mean(x * x, axis=-1, keepdims=True) + NORM_EPS)


def _bf16_table(a):
    return jnp.asarray(a, F32).astype(BF16)


def _ada_kernel(cond_ref, w_ref, b_ref, o_ref):
    o_ref[...] = _dot3(_silu(cond_ref[...]), w_ref[...]) + b_ref[...]


def _ada(cond, w, b):
    d = cond.shape[1]
    n = w.shape[1]
    return pl.pallas_call(
        _ada_kernel,
        out_shape=jax.ShapeDtypeStruct((cond.shape[0], n), F32),
        grid=(n // d,),
        in_specs=[pl.BlockSpec(cond.shape, lambda j: (0, 0)),
                  pl.BlockSpec((d, d), lambda j: (0, j)),
                  pl.BlockSpec((1, d), lambda j: (0, j))],
        out_specs=pl.BlockSpec((cond.shape[0], d), lambda j: (0, j)),
        compiler_params=pltpu.CompilerParams(vmem_limit_bytes=VMEM_LIMIT),
        name="ada",
    )(cond, w, b.reshape(1, n))


def _inproj_kernel(x_ref, mod_ref, gmix_ref, w_ref, lbl_ref, *out_refs, ctx_row, parts):
    d = x_ref.shape[2]
    row = ctx_row if ctx_row is not None else pl.program_id(0)
    shift = mod_ref[pl.ds(row, 1), 0:d]
    scale = mod_ref[pl.ds(row, 1), d:2 * d]
    l0 = lbl_ref[0]
    l1 = lbl_ref[1]
    m = jnp.maximum(l0, l1)
    e0 = jnp.exp(l0 - m)
    e1 = jnp.exp(l1 - m)
    lb = e0 / (e0 + e1)

    n_groups = INPROJ_ROW_GROUPS if x_ref.shape[1] % (INPROJ_ROW_GROUPS * SUBLANE) == 0 else 1
    rows_per_group = x_ref.shape[1] // n_groups
    for grp in range(n_groups):
        rows = slice(grp * rows_per_group, (grp + 1) * rows_per_group)
        h = _rms(x_ref[0, rows, :]) * gmix_ref[...] * (1.0 + scale) + shift
        hb = h.astype(BF16)
        for o_ref, (kind, col, _) in zip(out_refs, parts):
            w = o_ref.shape[2]
            p = _dot(hb, w_ref[:, col:col + w])
            if kind == "raw":
                o_ref[0, rows, :] = p.astype(o_ref.dtype)
            elif kind == "silu":
                o_ref[0, rows, :] = _silu(p).astype(o_ref.dtype)
            else:
                lbd = lb[0:1] if kind == "logf_fwd" else lb[1:2]
                o_ref[0, rows, :] = jnp.log(lbd + (1.0 - lbd) * jax.nn.sigmoid(p))


def _inproj(x, mod, g_mix, w_bf16, lb_logits, *, ctx_row, parts, tm):
    b, l, d = x.shape
    width = 512
    kern = functools.partial(_inproj_kernel, ctx_row=ctx_row, parts=parts)
    return pl.pallas_call(
        kern,
        out_shape=[jax.ShapeDtypeStruct((b, l, width), dt) for _, _, dt in parts],
        grid=(b, l // tm),
        in_specs=[pl.BlockSpec((1, tm, d), lambda i, j: (i, j, 0)),
                  pl.BlockSpec(mod.shape, lambda i, j: (0, 0)),
                  pl.BlockSpec((1, d), lambda i, j: (0, 0)),
                  pl.BlockSpec(w_bf16.shape, lambda i, j: (0, 0)),
                  pl.BlockSpec(lb_logits.shape, lambda i, j: (0, 0, 0))],
        out_specs=[pl.BlockSpec((1, tm, width), lambda i, j: (i, j, 0)) for _ in parts],
        compiler_params=pltpu.CompilerParams(
            dimension_semantics=("parallel", "parallel"), vmem_limit_bytes=VMEM_LIMIT),
        name="inproj_ctx" if ctx_row is not None else "inproj",
    )(x, mod, g_mix.reshape(1, d), w_bf16, lb_logits)


def _dft_tables():
    n1, n2 = DFT_ROWS, DFT_COLS
    k1 = np.arange(n1)
    a1 = 2.0 * np.pi * np.outer(k1, k1) / n1
    f1 = np.concatenate([np.cos(a1), -np.sin(a1)], axis=0)
    c = np.arange(n2)
    th = 2.0 * np.pi * np.outer(c, k1) / (n1 * n2)
    twc = np.repeat(np.cos(th)[:, :, None], LANE, axis=2)
    tws = np.repeat(np.sin(th)[:, :, None], LANE, axis=2)
    a2 = 2.0 * np.pi * np.outer(c, c) / n2
    f3a = np.concatenate([np.cos(a2), -np.sin(a2)], axis=0)
    f3b = np.concatenate([np.sin(a2), np.cos(a2)], axis=0)
    return (_bf16_table(f1), jnp.asarray(twc, F32), jnp.asarray(tws, F32),
            _bf16_table(f3a), _bf16_table(f3b))


def _dft1_kernel(*refs):
    n_slab = (len(refs) - 3) // 3
    x_refs = refs[:n_slab]
    f1_ref, twc_ref, tws_ref = refs[n_slab:n_slab + 3]
    re_refs = refs[n_slab + 3:2 * n_slab + 3]
    im_refs = refs[2 * n_slab + 3:]
    n1, cb = x_refs[0].shape[1:3]
    flat = lambda ref: ref.at[0].reshape(n1 * cb, LANE)
    xs, res, ims = [[flat(r) for r in group] for group in (x_refs, re_refs, im_refs)]
    for cc in range(cb):
        rows = pl.ds(cc, n1, stride=cb)
        xc = jnp.concatenate([x[rows, :] for x in xs], axis=1).astype(BF16)
        g = _dot(f1_ref[...], xc)
        tre = g[:n1]
        tim = g[n1:]
        cw = twc_ref[cc]
        sw = tws_ref[cc]
        for s in range(n_slab):
            cols = slice(s * LANE, (s + 1) * LANE)
            res[s][rows, :] = tre[:, cols] * cw + tim[:, cols] * sw
            ims[s][rows, :] = tim[:, cols] * cw - tre[:, cols] * sw


def _dft1(u4, f1, twc, tws):
    b, n1, n2, ch = u4.shape
    cb = SUBLANE
    n_slab = ch // LANE
    slab = lambda s: pl.BlockSpec((1, n1, cb, LANE), lambda i, j, s=s: (i, 0, j, s))
    return pl.pallas_call(
        _dft1_kernel,
        out_shape=[jax.ShapeDtypeStruct((b, n1, n2, LANE), F32)] * (2 * n_slab),
        grid=(b, n2 // cb),
        in_specs=[slab(s) for s in range(n_slab)] + [
            pl.BlockSpec(f1.shape, lambda i, j: (0, 0)),
            pl.BlockSpec((cb, n1, LANE), lambda i, j: (j, 0, 0)),
            pl.BlockSpec((cb, n1, LANE), lambda i, j: (j, 0, 0))],
        out_specs=[pl.BlockSpec((1, n1, cb, LANE), lambda i, j: (i, 0, j, 0))] * (2 * n_slab),
        compiler_params=pltpu.CompilerParams(
            dimension_semantics=("parallel", "parallel"), vmem_limit_bytes=VMEM_LIMIT),
        name="dft1",
    )(*([u4] * n_slab), f1, twc, tws)


def _dft3_kernel(*refs):
    n_slab = (len(refs) - 3) // 2
    re_refs = refs[:n_slab]
    im_refs = refs[n_slab:2 * n_slab]
    f3a_ref, f3b_ref, o_ref = refs[2 * n_slab:]
    n2 = re_refs[0].shape[2]
    ch = n_slab * LANE
    for kk in range(re_refs[0].shape[1]):
        tre = jnp.concatenate([r[0, kk] for r in re_refs], axis=1).astype(BF16)
        tim = jnp.concatenate([r[0, kk] for r in im_refs], axis=1).astype(BF16)
        x = _dot(f3a_ref[...], tre) + _dot(f3b_ref[...], tim)
        o_ref[0, :, kk, 0:ch] = x[:n2]
        o_ref[0, :, kk, ch:2 * ch] = x[n2:]


def _dft3(t_slabs, f3a, f3b):
    n_slab = len(t_slabs) // 2
    b, n1, n2, _ = t_slabs[0].shape
    w = 2 * n_slab * LANE
    kb = SUBLANE
    return pl.pallas_call(
        _dft3_kernel,
        out_shape=jax.ShapeDtypeStruct((b, n2, n1, w), F32),
        grid=(b, n1 // kb),
        in_specs=[pl.BlockSpec((1, kb, n2, LANE), lambda i, j: (i, j, 0, 0))] * (2 * n_slab) + [
            pl.BlockSpec(f3a.shape, lambda i, j: (0, 0)),
            pl.BlockSpec(f3b.shape, lambda i, j: (0, 0))],
        out_specs=pl.BlockSpec((1, n2, kb, w), lambda i, j: (i, 0, j, 0)),
        compiler_params=pltpu.CompilerParams(
            dimension_semantics=("parallel", "parallel"), vmem_limit_bytes=VMEM_LIMIT),
        name="dft3",
    )(*t_slabs, f3a, f3b)


def _rec_tables(reverse):
    c = REC_CHUNK
    t = np.arange(c)[:, None]
    s = np.arange(c)[None, :]
    masks = [np.eye(c)]
    m = 1
    while m < c:
        masks.append(((t // (2 * m) == s // (2 * m)) & ((t // m) % 2 == 1) & ((s // m) % 2 == 0)).astype(np.float64))
        m *= 2
    m = 4
    while m < c:
        masks.append(np.where((t // m) % 2 == 1, 1.0, -1.0) * np.ones((1, c)))
        m *= 2
    cum = (s <= t).astype(np.float64)
    if reverse:
        cum = cum[::-1, ::-1]
        masks = [a[::-1, ::-1] for a in masks]
    return jnp.asarray(cum, BF16), jnp.asarray(np.stack(masks), F32)


def _level_logdecay(b, lc, m, reverse, mask_ref):
    c, d = lc.shape
    if m <= 2:
        row = lax.broadcasted_iota(I32, (c, d), 0)
        if m == 1:
            return jnp.where((row & 1) == (0 if reverse else 1), lc, 0.0)
        r = row & 3
        nxt = pltpu.roll(lc, c - 1, 0)
        prv = pltpu.roll(lc, 1, 0)
        if reverse:
            return jnp.where(r == 0, lc + nxt, jnp.where(r == 1, lc, jnp.where(r == 2, 0.0, prv)))
        return jnp.where(r == 0, nxt, jnp.where(r == 1, 0.0, jnp.where(r == 2, lc, lc + prv)))
    anchor_row = m if reverse else m - 1
    b3 = b.reshape(c // (2 * m), 2 * m, d)
    anchor = jnp.broadcast_to(b3[:, anchor_row:anchor_row + 1, :], b3.shape).reshape(c, d)
    n_pair_planes = 1 + (c.bit_length() - 1)
    sign = mask_ref[n_pair_planes + (m.bit_length() - 1) - 2, :, 0:d]
    return (b - anchor) * sign


def _rec_direction(q, v, lc, st_ref, h, cum_ref, mask_ref, emit_out, reverse):
    c = REC_CHUNK
    lc = lc * LOG2_E
    k = 1.0 - jnp.exp2(lc)
    hi, lo = _split(lc)
    res = _dot(cum_ref[...], jnp.concatenate([hi, lo], axis=1))
    b = res[:, :HEAD_DIM] + res[:, HEAD_DIM:]
    last = 0 if reverse else c - 1
    b_last = b[last:last + 1, :]
    st = st_ref[h]
    kb = k.astype(BF16)
    o = None
    if emit_out:
        half = c // 2
        lo_, hi_ = slice(0, half), slice(half, c)
        p = _dot_nt(q, kb)
        a_lo = mask_ref[0, lo_, lo_] * p[lo_, lo_]
        a_hi = mask_ref[0, hi_, hi_] * p[hi_, hi_]
        m, i = 1, 1
        while m < half:
            dm = jnp.exp2(_level_logdecay(b, lc, m, reverse, mask_ref)).astype(BF16)
            p = _dot_nt(q * dm, kb * dm)
            a_lo = a_lo + mask_ref[i, lo_, lo_] * p[lo_, lo_]
            a_hi = a_hi + mask_ref[i, hi_, hi_] * p[hi_, hi_]
            m, i = 2 * m, i + 1
        dm = jnp.exp2(_level_logdecay(b, lc, half, reverse, mask_ref)).astype(BF16)
        late, early = (lo_, hi_) if reverse else (hi_, lo_)
        cross = _dot_nt(q[late] * dm[late], kb[early] * dm[early])
        zero = jnp.zeros((half, half), F32)
        if reverse:
            a = jnp.concatenate([jnp.concatenate([a_lo, cross], axis=1),
                                 jnp.concatenate([zero, a_hi], axis=1)], axis=0)
        else:
            a = jnp.concatenate([jnp.concatenate([a_lo, zero], axis=1),
                                 jnp.concatenate([cross, a_hi], axis=1)], axis=0)
        qd = q * jnp.exp2(b).astype(BF16)
        o = _dot_nt(qd, st.astype(BF16)) + _dot(a.astype(BF16), v)
    kd = kb * jnp.exp2(b_last - b).astype(BF16)
    st_ref[h] = st * jnp.exp2(b_last) + _dot_tn(v, kd)
    return o


def _rec_kernel(qf_ref, vf_ref, lf_ref, qb_ref, vb_ref, lb_ref, s0f_ref, s0b_ref,
                wf_ref, mf_ref, wb_ref, mb_ref, of_ref, ob_ref, sf_ref, sb_ref, *, emit_out):
    n = pl.program_id(1)

    @pl.when(n == 0)
    def _():
        sf_ref[...] = s0f_ref[0]
        sb_ref[...] = s0b_ref[0]

    for h in range(REC_HEADS):
        cols = slice(h * HEAD_DIM, (h + 1) * HEAD_DIM)
        o_f = _rec_direction(qf_ref[0, :, cols], vf_ref[0, :, cols], lf_ref[0, :, cols],
                             sf_ref, h, wf_ref, mf_ref, emit_out, False)
        o_b = _rec_direction(qb_ref[0, :, cols], vb_ref[0, :, cols], lb_ref[0, :, cols],
                             sb_ref, h, wb_ref, mb_ref, emit_out, True)
        if emit_out:
            of_ref[0, :, cols] = o_f.astype(of_ref.dtype)
            ob_ref[0, :, cols] = o_b.astype(ob_ref.dtype)
    if not emit_out:
        @pl.when(n == pl.num_programs(1) - 1)
        def _():
            of_ref[0] = sf_ref[...]
            ob_ref[0] = sb_ref[...]


def _rec(q, v, lcf, lcb, s0f, s0b, *, emit_out):
    b, l, w = q.shape
    c = REC_CHUNK
    n = l // c
    wf, mf = _rec_tables(False)
    wb, mb = _rec_tables(True)
    fwd = lambda i, j: (i, j, 0)
    bwd = lambda i, j: (i, n - 1 - j, 0)
    tok = lambda im: pl.BlockSpec((1, c, w), im)
    state = pl.BlockSpec((1, REC_HEADS, HEAD_DIM, HEAD_DIM), lambda i, j: (i, 0, 0, 0))
    const2 = lambda a: pl.BlockSpec(a.shape, lambda i, j: (0, 0))
    const3 = lambda a: pl.BlockSpec(a.shape, lambda i, j: (0, 0, 0))
    if emit_out:
        out_shape = [jax.ShapeDtypeStruct((b, l, w), BF16)] * 2
        out_specs = [tok(fwd), tok(bwd)]
    else:
        out_shape = [jax.ShapeDtypeStruct((b, REC_HEADS, HEAD_DIM, HEAD_DIM), F32)] * 2
        out_specs = [state, state]
    state_scratch = pltpu.VMEM((REC_HEADS, HEAD_DIM, HEAD_DIM), F32)
    return pl.pallas_call(
        functools.partial(_rec_kernel, emit_out=emit_out),
        out_shape=out_shape,
        grid=(b, n),
        in_specs=[tok(fwd), tok(fwd), tok(fwd), tok(bwd), tok(bwd), tok(bwd), state, state,
                  const2(wf), const3(mf), const2(wb), const3(mb)],
        out_specs=out_specs,
        scratch_shapes=[state_scratch, state_scratch],
        compiler_params=pltpu.CompilerParams(
            dimension_semantics=("parallel", "arbitrary"), vmem_limit_bytes=VMEM_LIMIT),
        name="rec" if emit_out else "rec_ctx",
    )(q, v, lcf, q, v, lcb, s0f, s0b, wf, mf, wb, mb)


def _channel_dft_tables(scale):
    k = np.arange(GROUP_DIM)
    a = 2.0 * np.pi * np.outer(k, k) / GROUP_DIM
    return _bf16_table(np.cos(a) * scale), _bf16_table(np.sin(a) * scale)


def _outproj_kernel(xs_ref, of_ref, ob_ref, sg_ref, x_ref, mod_ref, grec_ref, cc_ref, sc_ref,
                    wf_ref, wo_ref, gffn_ref, wr_ref, x1_ref, h2_ref, aff_ref, *, n_experts):
    d = x_ref.shape[2]
    half = xs_ref.shape[2] // 2
    row = pl.program_id(0)
    gate1 = mod_ref[pl.ds(row, 1), 2 * d:3 * d]
    shift2 = mod_ref[pl.ds(row, 1), 3 * d:4 * d]
    scale2 = mod_ref[pl.ds(row, 1), 4 * d:5 * d]

    rows_per_group = x_ref.shape[1] // OUTPROJ_ROW_GROUPS
    for grp in range(OUTPROJ_ROW_GROUPS):
        rows = slice(grp * rows_per_group, (grp + 1) * rows_per_group)
        parts = []
        for g in range(FOURIER_GROUPS):
            lo = g * GROUP_DIM
            xre = xs_ref[0, rows, lo:lo + GROUP_DIM].astype(BF16)
            xim = xs_ref[0, rows, half + lo:half + lo + GROUP_DIM].astype(BF16)
            rez = _dot(xre, cc_ref[...]) + _dot(xim, sc_ref[...])
            parts.append(_dot(rez.astype(BF16), wf_ref[g]).astype(BF16))
        o = of_ref[0, rows, :].astype(F32) + ob_ref[0, rows, :].astype(F32)
        for h in range(REC_HEADS):
            lo = h * HEAD_DIM
            oh = _rms(o[:, lo:lo + HEAD_DIM]) * grec_ref[:, lo:lo + HEAD_DIM]
            parts.append((oh * sg_ref[0, rows, lo:lo + HEAD_DIM]).astype(BF16))
        mix = _dot(jnp.concatenate(parts, axis=1), wo_ref[...])
        x1 = x_ref[0, rows, :] + gate1 * mix
        x1_ref[0, rows, :] = x1

        h2 = _rms(x1) * gffn_ref[...] * (1.0 + scale2) + shift2
        h2_ref[0, rows, :] = h2
        logits = _dot3(h2, wr_ref[...])
        lane = lax.broadcasted_iota(I32, logits.shape, 1)
        logits = jnp.where(lane < n_experts, logits, -1e30)
        e = jnp.exp(logits - jnp.max(logits, axis=-1, keepdims=True))
        aff = e / jnp.sum(e, axis=-1, keepdims=True)
        aff_ref[0, :, rows] = aff.T[:n_experts]


def _outproj(xs, o_f, o_b, sg, x, mod, g_rec, w_fourier_bf16, w_out_bf16, g_ffn, w_router_pad, *, n_experts, tm):
    b, l, d = x.shape
    cc, sc = _channel_dft_tables(1.0 / np.sqrt(l * GROUP_DIM))
    tokens = lambda w: pl.BlockSpec((1, tm, w), lambda i, j: (i, j, 0))
    full = lambda a: pl.BlockSpec(a.shape, lambda i, j: (0,) * a.ndim)
    g_rec2 = g_rec.reshape(1, -1)
    g_ffn2 = g_ffn.reshape(1, d)
    return pl.pallas_call(
        functools.partial(_outproj_kernel, n_experts=n_experts),
        out_shape=[jax.ShapeDtypeStruct((b, l, d), F32),
                   jax.ShapeDtypeStruct((b, l, d), F32),
                   jax.ShapeDtypeStruct((b, n_experts, l), F32)],
        grid=(b, l // tm),
        in_specs=[tokens(xs.shape[2]), tokens(o_f.shape[2]), tokens(o_b.shape[2]), tokens(sg.shape[2]),
                  tokens(d), full(mod), full(g_rec2), full(cc), full(sc), full(w_fourier_bf16),
                  full(w_out_bf16), full(g_ffn2), full(w_router_pad)],
        out_specs=[tokens(d),
                   tokens(d),
                   pl.BlockSpec((1, n_experts, tm), lambda i, j: (i, 0, j))],
        compiler_params=pltpu.CompilerParams(
            dimension_semantics=("parallel", "parallel"), vmem_limit_bytes=VMEM_LIMIT),
        name="outproj",
    )(xs, o_f, o_b, sg, x, mod, g_rec2, cc, sc, w_fourier_bf16, w_out_bf16, g_ffn2, w_router_pad)


def _route_kernel(aff_ref, tri_ref, idx_ref, cum_ref, *, cap):
    a = aff_ref[0]
    ne, l = a.shape

    def value_bit(i, thr):
        cand = thr | jnp.left_shift(jnp.int32(1), 30 - i)
        cnt = jnp.sum(jnp.where(a >= pltpu.bitcast(cand, F32), 1.0, 0.0), axis=1, keepdims=True)
        return jnp.where(cnt >= cap, cand, thr)

    thr = lax.fori_loop(0, 31, value_bit, jnp.zeros((ne, 1), I32))
    subnormal = thr < MIN_NORMAL_BITS
    lo = jnp.where(subnormal, 0, thr)
    hi = jnp.where(subnormal, MIN_NORMAL_BITS, thr + 1)
    gt = a >= pltpu.bitcast(hi, F32)
    eq = (a >= pltpu.bitcast(lo, F32)) & jnp.logical_not(gt)
    need = cap - jnp.sum(jnp.where(gt, 1.0, 0.0), axis=1, keepdims=True)
    pos = lax.broadcasted_iota(I32, (ne, l), 1)
    nbits = int(l).bit_length()

    def index_bit(i, x):
        cand = x + jnp.left_shift(jnp.int32(1), nbits - 1 - i)
        cnt = jnp.sum(jnp.where(eq & (pos < cand), 1.0, 0.0), axis=1, keepdims=True)
        return jnp.where(cnt < need, cand, x)

    last = lax.fori_loop(0, nbits, index_bit, jnp.zeros((ne, 1), I32))
    sel = (gt | (eq & (pos <= last))).astype(BF16)

    n_tiles = l // LANE
    cum_ref[:, n_tiles:, :] = jnp.zeros((ne, LANE - n_tiles, LANE), F32)
    lane = lax.broadcasted_iota(I32, (ne, LANE), 1)
    off = jnp.zeros((ne, 1), F32)
    ends = jnp.zeros((ne, LANE), F32)
    for c in range(n_tiles):
        p = _dot(sel[:, c * LANE:(c + 1) * LANE], tri_ref[...]) + off
        cum_ref[:, c, :] = p
        off = p[:, LANE - 1:LANE]
        ends = jnp.where(lane == c, off, ends)
    never = float(2 * l)
    ends = jnp.where(lane < n_tiles, ends, never)
    starts = jnp.where(lane == 0, 0.0, pltpu.roll(ends, 1, 1))
    ones = jnp.ones((8, LANE), BF16)
    slot = lax.broadcasted_iota(I32, (cap, LANE), 0).astype(F32)
    for e in range(ne):
        en = ends[e:e + 1, :]
        st = starts[e:e + 1, :]
        before = en <= slot
        holds = ((st <= slot) & (slot < en)).astype(BF16)
        hi, lo = _split(cum_ref[e])
        tile_cum = _dot(holds, hi) + _dot(holds, lo)
        counts = jnp.where(before, float(LANE), 0.0) + jnp.where(tile_cum <= slot, 1.0, 0.0)
        tok = _dot_nt(ones, counts.astype(BF16))
        idx_ref[0, e:e + 1, :] = tok[0:1].astype(I32)


def _route(aff, cap):
    b, ne, l = aff.shape
    assert l % LANE == 0 and l // LANE <= LANE
    tri = jnp.asarray(np.triu(np.ones((LANE, LANE))), BF16)
    return pl.pallas_call(
        functools.partial(_route_kernel, cap=cap),
        out_shape=jax.ShapeDtypeStruct((b, ne, cap), I32),
        grid=(b,),
        in_specs=[pl.BlockSpec((1, ne, l), lambda i: (i, 0, 0)),
                  pl.BlockSpec(tri.shape, lambda i: (0, 0))],
        out_specs=pl.BlockSpec((1, ne, cap), lambda i: (i, 0, 0)),
        scratch_shapes=[pltpu.VMEM((ne, LANE, LANE), F32)],
        compiler_params=pltpu.CompilerParams(
            dimension_semantics=("parallel",), vmem_limit_bytes=VMEM_LIMIT),
        name="route",
    )(aff, tri)


def _ffn_kernel(src_ref, h_hbm, wg_ref, wu_ref, wd_ref, y_ref, xg_ref, xs_ref, sem, *, ne, nb, nf):
    e = pl.program_id(0)
    f = pl.program_id(1)
    n_used = xs_ref.shape[0]
    rows_per_step = xg_ref.shape[2]
    n_rows = rows_per_step * nf
    slot = e % 2

    def start_row(expert, slot_, step, i):
        t = src_ref[expert * n_rows + step * rows_per_step + i]
        pltpu.make_async_copy(h_hbm.at[pl.ds(t, 1)], xg_ref.at[slot_, step, pl.ds(i, 1)], sem.at[slot_]).start()

    def wait_rows(slot_):
        for step in range(nf):
            pltpu.make_async_copy(h_hbm.at[pl.ds(0, rows_per_step)], xg_ref.at[slot_, step], sem.at[slot_]).wait()

    @pl.when((e == 0) & (f == 0))
    def _():
        for step in range(nf):
            def issue(i, carry, step=step):
                start_row(0, 0, step, i)
                return carry

            lax.fori_loop(0, rows_per_step, issue, 0)

    @pl.when(f == 0)
    def _():
        wait_rows(slot)
        for step in range(nf):
            lo = step * rows_per_step
            cnt = min(rows_per_step, n_used - lo)
            if cnt > 0:
                xs_ref[lo:lo + cnt, :] = xg_ref[slot, step, 0:cnt, :].astype(BF16)
        y_ref[...] = jnp.zeros(y_ref.shape, F32)

    nxt = jnp.minimum(e + 1, ne - 1)
    for i in range(rows_per_step):
        start_row(nxt, 1 - slot, f, i)

    cap = n_used // nb

    wg = wg_ref[0].astype(BF16)
    wu = wu_ref[0].astype(BF16)
    wd = wd_ref[0].astype(BF16)
    for gb in range(nb):
        xs = xs_ref[gb * cap:(gb + 1) * cap, :]
        hid = _silu(_dot(xs, wg)) * _dot(xs, wu)
        y_ref[gb, 0] += _dot(hid.astype(BF16), wd)

    @pl.when((e == ne - 1) & (f == nf - 1))
    def _():
        wait_rows(1 - slot)


def _ffn(idx, h_rows, w_gate, w_up, w_down, *, tf):
    batch, _, cap = idx.shape
    ne, d, dff = w_gate.shape
    n_tokens = h_rows.shape[0] // batch
    nf = dff // tf
    rows_per_step = -(-(batch * cap) // (SUBLANE * nf)) * SUBLANE
    n_rows = rows_per_step * nf
    src = idx + (jnp.arange(batch, dtype=I32) * n_tokens)[:, None, None]
    src = jnp.transpose(src, (1, 0, 2)).reshape(ne, batch * cap)
    src = jnp.pad(src, ((0, 0), (0, n_rows - batch * cap)), mode="edge").reshape(-1)
    return pl.pallas_call(
        functools.partial(_ffn_kernel, ne=ne, nb=batch, nf=nf),
        out_shape=jax.ShapeDtypeStruct((batch, ne, cap, d), F32),
        grid_spec=pltpu.PrefetchScalarGridSpec(
            num_scalar_prefetch=1,
            grid=(ne, nf),
            in_specs=[pl.BlockSpec(memory_space=pl.ANY),
                      pl.BlockSpec((1, d, tf), lambda e, f, idx: (e, 0, f)),
                      pl.BlockSpec((1, d, tf), lambda e, f, idx: (e, 0, f)),
                      pl.BlockSpec((1, tf, d), lambda e, f, idx: (e, f, 0))],
            out_specs=pl.BlockSpec((batch, 1, cap, d), lambda e, f, idx: (0, e, 0, 0)),
            scratch_shapes=[pltpu.VMEM((2, nf, rows_per_step, d), F32),
                            pltpu.VMEM((batch * cap, d), BF16),
                            pltpu.SemaphoreType.DMA((2,))],
        ),
        compiler_params=pltpu.CompilerParams(
            dimension_semantics=("arbitrary", "arbitrary"), vmem_limit_bytes=VMEM_LIMIT),
        name="ffn",
    )(src, h_rows, w_gate, w_up, w_down)


COMBINE_GROUP = SUBLANE
NORM_ROWS = 256


def _combine_kernel(idx_ref, aff_ref, y_ref, x1_hbm, mod_ref, g_ref, o_hbm, acc_ref, sem):
    b = pl.program_id(0)
    e = pl.program_id(1)
    ne = pl.num_programs(1)
    cap = y_ref.shape[2]
    n_tokens, d = acc_ref.shape
    base = (b * ne + e) * cap
    gate2 = mod_ref[pl.ds(b, 1), 5 * d:6 * d]

    @pl.when(e == 0)
    def _():
        load = pltpu.make_async_copy(x1_hbm.at[b], acc_ref, sem)
        load.start()
        load.wait()

    def add_rows(g, carry):
        j0 = pl.multiple_of(g * COMBINE_GROUP, COMBINE_GROUP)
        for u in range(COMBINE_GROUP):
            t = idx_ref[base + j0 + u]
            acc_ref[pl.ds(t, 1), :] += y_ref[0, 0, pl.ds(j0 + u, 1), :] * aff_ref[0, 0, t] * gate2
        return carry

    lax.fori_loop(0, cap // COMBINE_GROUP, add_rows, 0)

    @pl.when(e == ne - 1)
    def _():
        def norm_rows(i, carry):
            rows = pl.ds(pl.multiple_of(i * NORM_ROWS, NORM_ROWS), NORM_ROWS)
            acc_ref[rows, :] = _rms(acc_ref[rows, :]) * g_ref[...]
            return carry

        lax.fori_loop(0, n_tokens // NORM_ROWS, norm_rows, 0)
        out = pltpu.make_async_copy(acc_ref, o_hbm.at[b], sem)
        out.start()
        out.wait()


def _combine(idx_flat, aff_rows, ye, x1, mod, g_final):
    b, ne, cap, d = ye.shape
    n_tokens = x1.shape[1]
    assert cap % COMBINE_GROUP == 0 and n_tokens % NORM_ROWS == 0
    return pl.pallas_call(
        _combine_kernel,
        out_shape=jax.ShapeDtypeStruct((b, n_tokens, d), F32),
        grid_spec=pltpu.PrefetchScalarGridSpec(
            num_scalar_prefetch=1,
            grid=(b, ne),
            in_specs=[pl.BlockSpec((1, 1, n_tokens), lambda i, e, idx: (i * ne + e, 0, 0),
                                   memory_space=pltpu.SMEM),
                      pl.BlockSpec((1, 1, cap, d), lambda i, e, idx: (i, e, 0, 0)),
                      pl.BlockSpec(memory_space=pl.ANY),
                      pl.BlockSpec(mod.shape, lambda i, e, idx: (0, 0)),
                      pl.BlockSpec((1, d), lambda i, e, idx: (0, 0))],
            out_specs=pl.BlockSpec(memory_space=pl.ANY),
            scratch_shapes=[pltpu.VMEM((n_tokens, d), F32), pltpu.SemaphoreType.DMA],
        ),
        compiler_params=pltpu.CompilerParams(
            dimension_semantics=("arbitrary", "arbitrary"), vmem_limit_bytes=VMEM_LIMIT),
        name="combine",
    )(idx_flat, aff_rows, ye, x1, mod, g_final.reshape(1, d))


def kernel(x, c, ctx, c_ctx, w_ada, b_ada, g_mix, w_in, w_fourier, lb_logits, g_rec, w_out, g_ffn, w_router, w_exp_gate, w_exp_up, w_exp_down, g_final):
    b, l, d = x.shape
    depth = w_ada.shape[0]
    assert depth == 1 and lb_logits.shape[0] == 2 and l == DFT_ROWS * DFT_COLS
    d_f = FOURIER_GROUPS * GROUP_DIM
    d_r = REC_HEADS * HEAD_DIM
    ne = w_router.shape[2]
    cap = EC_CAPACITY_FACTOR * l // ne

    cond = jnp.concatenate([c, c_ctx[None], jnp.zeros((8 - b - 1, d), F32)], axis=0)
    mod = _ada(cond, w_ada[0], b_ada[0])

    w_in_b = w_in[0].astype(BF16)
    col = lambda i: d_f + i * d_r
    lat_parts = (("raw", 0, F32), ("silu", col(0), BF16), ("raw", col(1), BF16), ("logf_fwd", col(2), F32),
                 ("logf_bwd", col(3), F32), ("silu", col(4), BF16))
    u, q, v, lcf, lcb, sg = _inproj(x, mod, g_mix[0], w_in_b, lb_logits, ctx_row=None, parts=lat_parts, tm=512)
    q_c, v_c, lcf_c, lcb_c = _inproj(ctx, mod, g_mix[0], w_in_b, lb_logits, ctx_row=b, parts=lat_parts[1:5],
                                     tm=ctx.shape[1])

    f1, twc, tws, f3a, f3b = _dft_tables()
    t4 = _dft1(u.reshape(b, DFT_ROWS, DFT_COLS, d_f), f1, twc, tws)
    xs = _dft3(t4, f3a, f3b).reshape(b, l, 2 * d_f)

    zero = jnp.zeros((b, REC_HEADS, HEAD_DIM, HEAD_DIM), F32)
    s_f, s_b = _rec(q_c, v_c, lcf_c, lcb_c, zero, zero, emit_out=False)
    o_f, o_b = _rec(q, v, lcf, lcb, s_f, s_b, emit_out=True)

    w_router_pad = jnp.pad(w_router[0], ((0, 0), (0, LANE - ne)))
    x1, h2, aff = _outproj(xs, o_f, o_b, sg, x, mod, g_rec[0], w_fourier[0].astype(BF16),
                           w_out[0].astype(BF16), g_ffn[0], w_router_pad, n_experts=ne, tm=512)

    idx = _route(aff, cap)
    ye = _ffn(idx, h2.reshape(b * l, d), w_exp_gate[0], w_exp_up[0], w_exp_down[0], tf=256)
    return _combine(idx.reshape(-1), aff.reshape(b * ne, 1, l), ye, x1, mod, g_final)
```
